```python
import math
import jax, jax.numpy as jnp
from jax import lax
import numpy as np

D_MODEL = 1024
BATCH = 8
SEQ = 2048
DEPTH = 4
DEC_BATCH = 128
DEC_SEQ = 1
PAST_LEN = 8192
PAGE_SIZE = 128

N_MIXERS = 2
N_DIFF_LAYERS = (DEPTH + 1) // 2
N_SWA_LAYERS = DEPTH // 2

DIFF_HEAD_DIM = 64
DIFF_HEADS = D_MODEL // (2 * DIFF_HEAD_DIM)
DIFF_KV_HEADS = 2
DIFF_GROUP = DIFF_HEADS // DIFF_KV_HEADS
DIFF_Q_DIM = DIFF_HEADS * 2 * DIFF_HEAD_DIM
DIFF_K_DIM = DIFF_KV_HEADS * 2 * DIFF_HEAD_DIM
DIFF_V_DIM = DIFF_KV_HEADS * 2 * DIFF_HEAD_DIM
DIFF_QKV_DIM = DIFF_Q_DIM + DIFF_K_DIM + DIFF_V_DIM
DIFF_O_DIM = DIFF_HEADS * 2 * DIFF_HEAD_DIM
Q_BLOCK = 128

SWA_HEAD_DIM = 64
SWA_HEADS = D_MODEL // SWA_HEAD_DIM
SWA_KV_HEADS = 2
SWA_GROUP = SWA_HEADS // SWA_KV_HEADS
SWA_Q_DIM = SWA_HEADS * SWA_HEAD_DIM
SWA_KV_DIM = SWA_KV_HEADS * SWA_HEAD_DIM
SWA_QKV_DIM = SWA_Q_DIM + 2 * SWA_KV_DIM
WINDOW = 128

N_EXPERTS = 32
TOP_K = 4
D_FF = D_MODEL
SWIGLU_ALPHA = 1.702
SWIGLU_LIMIT = 7.0
MOE_BLOCK = 128

ALPHA = (2 * DEPTH) ** 0.25
BETA = (8 * DEPTH) ** -0.25
LN_EPS = 1e-5
NEG_INF = -1e30

kernel_name = 'hybrid_diffattn_swa_sink_moe_decoder_step'


def layer_norm(x, g, b):
    xf = x.astype(jnp.float32)
    mu = jnp.mean(xf, axis=-1, keepdims=True)
    var = jnp.mean(jnp.square(xf - mu), axis=-1, keepdims=True)
    y = (xf - mu) * lax.rsqrt(var + LN_EPS) * g.astype(jnp.float32) + b.astype(jnp.float32)
    return y.astype(x.dtype)


def alibi_slopes(n_heads):
    return 2.0 ** (-8.0 * jnp.arange(1, n_heads + 1, dtype=jnp.float32) / n_heads)


def lambda_init(layer):
    return 0.8 - 0.6 * math.exp(-0.3 * layer)


def diff_lambda_value(lam_p, layer):
    lp = lam_p.astype(jnp.float32)
    return jnp.exp(jnp.sum(lp[0] * lp[1])) - jnp.exp(jnp.sum(lp[2] * lp[3])) + lambda_init(layer)


def diff_project(x, w_qkv):
    b, t, _ = x.shape
    h = x @ w_qkv
    q = h[..., :DIFF_Q_DIM].reshape(b, t, DIFF_KV_HEADS, DIFF_GROUP, 2, DIFF_HEAD_DIM)
    k = h[..., DIFF_Q_DIM:DIFF_Q_DIM + DIFF_K_DIM].reshape(b, t, DIFF_KV_HEADS, 2, DIFF_HEAD_DIM)
    v = h[..., DIFF_Q_DIM + DIFF_K_DIM:].reshape(b, t, DIFF_KV_HEADS, 2 * DIFF_HEAD_DIM)
    return q, k, v


def diff_attend(q, k, v, q_pos, k_pos, lam):
    s = jnp.einsum('bqkgcd,bskcd->bkgcqs', q, k).astype(jnp.float32) * (DIFF_HEAD_DIM ** -0.5)
    dist = (q_pos[:, None] - k_pos[None, :]).astype(jnp.float32)
    slopes = alibi_slopes(DIFF_HEADS).reshape(DIFF_KV_HEADS, DIFF_GROUP)[:, :, None, None, None]
    s = jnp.where(dist >= 0, s - slopes * dist, NEG_INF)
    a = jax.nn.softmax(s, axis=-1)
    p = a[:, :, :, 0] - lam * a[:, :, :, 1]
    return jnp.einsum('bkgqs,bskv->bqkgv', p.astype(v.dtype), v)


def diff_output(o, subln_g, w_o, layer):
    b, t = o.shape[:2]
    of = o.astype(jnp.float32)
    of = of * lax.rsqrt(jnp.mean(jnp.square(of), axis=-1, keepdims=True) + LN_EPS)
    of = of * subln_g.astype(jnp.float32) * (1.0 - lambda_init(layer))
    return of.reshape(b, t, DIFF_O_DIM).astype(o.dtype) @ w_o


def diff_prompt(x, w_qkv, lam_p, subln_g, w_o, layer):
    b, s_len, _ = x.shape
    q, k, v = diff_project(x, w_qkv)
    lam = diff_lambda_value(lam_p, layer)
    nq = s_len // Q_BLOCK
    pos = jnp.arange(s_len)
    q_blocks = jnp.moveaxis(q.reshape(b, nq, Q_BLOCK, *q.shape[2:]), 1, 0)
    q_pos_blocks = pos.reshape(nq, Q_BLOCK)
    o = lax.map(lambda a: diff_attend(a[0], k, v, a[1], pos, lam), (q_blocks, q_pos_blocks))
    o = jnp.moveaxis(o, 0, 1).reshape(b, s_len, DIFF_KV_HEADS, DIFF_GROUP, 2 * DIFF_HEAD_DIM)
    return diff_output(o, subln_g, w_o, layer), k, v


def diff_sample(x, cache_k, cache_v, layer_slot, page_table, w_qkv, lam_p, subln_g, w_o, layer):
    bd, t, _ = x.shape
    q, k, v = diff_project(x, w_qkv)
    lam = diff_lambda_value(lam_p, layer)
    k_past = cache_k[layer_slot, page_table].reshape(bd, PAST_LEN, DIFF_KV_HEADS, 2, DIFF_HEAD_DIM)
    v_past = cache_v[layer_slot, page_table].reshape(bd, PAST_LEN, DIFF_KV_HEADS, 2 * DIFF_HEAD_DIM)
    k_all = jnp.concatenate([k_past.astype(k.dtype), k], axis=1)
    v_all = jnp.concatenate([v_past.astype(v.dtype), v], axis=1)
    q_pos = PAST_LEN + jnp.arange(t)
    k_pos = jnp.arange(PAST_LEN + t)
    o = diff_attend(q, k_all, v_all, q_pos, k_pos, lam)
    return diff_output(o, subln_g, w_o, layer), k, v


def swa_project(x, w_qkv, b_qkv):
    b, t, _ = x.shape
    h = x @ w_qkv + b_qkv
    q = h[..., :SWA_Q_DIM].reshape(b, t, SWA_KV_HEADS, SWA_GROUP, SWA_HEAD_DIM)
    k = h[..., SWA_Q_DIM:SWA_Q_DIM + SWA_KV_DIM].reshape(b, t, SWA_KV_HEADS, SWA_HEAD_DIM)
    v = h[..., SWA_Q_DIM + SWA_KV_DIM:].reshape(b, t, SWA_KV_HEADS, SWA_HEAD_DIM)
    return q, k, v


def swa_attend(q, k, v, q_pos, k_pos, sinks):
    s = jnp.einsum('bnqkgd,bnskd->bnkgqs', q, k).astype(jnp.float32) * (SWA_HEAD_DIM ** -0.5)
    dist = (q_pos[:, :, None] - k_pos[:, None, :]).astype(jnp.float32)
    valid = (dist >= 0) & (dist < WINDOW) & (k_pos[:, None, :] >= 0)
    slopes = alibi_slopes(SWA_HEADS).reshape(SWA_KV_HEADS, SWA_GROUP)[:, :, None, None]
    s = jnp.where(valid[:, None, None], s - slopes * dist[:, None, None], NEG_INF)
    sink = sinks.astype(jnp.float32).reshape(SWA_KV_HEADS, SWA_GROUP)[:, :, None, None]
    m = jnp.maximum(jnp.max(s, axis=-1, keepdims=True), sink)
    e = jnp.exp(s - m)
    p = e / (jnp.sum(e, axis=-1, keepdims=True) + jnp.exp(sink - m))
    return jnp.einsum('bnkgqs,bnskd->bnqkgd', p.astype(v.dtype), v)


def swa_prompt(x, w_qkv, b_qkv, sinks, w_o, b_o):
    b, s_len, _ = x.shape
    q, k, v = swa_project(x, w_qkv, b_qkv)
    nb = s_len // WINDOW

    def band(t):
        tp = jnp.pad(t, ((0, 0), (WINDOW, 0), (0, 0), (0, 0)))
        prev = tp[:, :s_len].reshape(b, nb, WINDOW, *t.shape[2:])
        cur = t.reshape(b, nb, WINDOW, *t.shape[2:])
        return jnp.concatenate([prev, cur], axis=2)

    q_pos = jnp.arange(s_len).reshape(nb, WINDOW)
    k_pos = jnp.concatenate([q_pos - WINDOW, q_pos], axis=1)
    o = swa_attend(q.reshape(b, nb, WINDOW, SWA_KV_HEADS, SWA_GROUP, SWA_HEAD_DIM),
                   band(k), band(v), q_pos, k_pos, sinks)
    y = o.reshape(b, s_len, SWA_Q_DIM) @ w_o + b_o
    keep = min(WINDOW, s_len)
    return y, k[:, s_len - keep:], v[:, s_len - keep:]


def swa_sample(x, buf_k, buf_v, w_qkv, b_qkv, sinks, w_o, b_o):
    bd, t, _ = x.shape
    q, k, v = swa_project(x, w_qkv, b_qkv)
    win_buf = buf_k.shape[1]
    k_all = jnp.concatenate([buf_k.astype(k.dtype), k], axis=1)
    v_all = jnp.concatenate([buf_v.astype(v.dtype), v], axis=1)
    q_pos = (PAST_LEN + jnp.arange(t))[None]
    k_pos = (PAST_LEN - win_buf + jnp.arange(win_buf + t))[None]
    o = swa_attend(q[:, None], k_all[:, None], v_all[:, None], q_pos, k_pos, sinks)
    y = o.reshape(bd, t, SWA_Q_DIM) @ w_o + b_o
    return y, k_all[:, -win_buf:], v_all[:, -win_buf:]


def moe(x, w_router, b_router, w_gu, b_gu, w_down, b_down):
    shp = x.shape
    xt = x.reshape(-1, D_MODEL)
    n_tok = xt.shape[0]
    logits = (xt @ w_router).astype(jnp.float32) + b_router.astype(jnp.float32)
    top_v, top_e = lax.top_k(logits, TOP_K)
    gates = jax.nn.softmax(top_v, axis=-1)
    n_asg = n_tok * TOP_K
    flat_e = top_e.reshape(-1)
    flat_tok = jnp.arange(n_asg, dtype=jnp.int32) // TOP_K
    flat_g = gates.reshape(-1)
    order = jnp.argsort(flat_e)
    sorted_e = flat_e[order]
    counts = jnp.bincount(flat_e, length=N_EXPERTS)
    padded = (counts + MOE_BLOCK - 1) // MOE_BLOCK * MOE_BLOCK
    pad_end = jnp.cumsum(padded)
    pad_start = pad_end - padded
    start = jnp.cumsum(counts) - counts
    dest = pad_start[sorted_e] + jnp.arange(n_asg, dtype=jnp.int32) - start[sorted_e]
    n_blocks = -(-n_asg // MOE_BLOCK) + N_EXPERTS
    buf_tok = jnp.full((n_blocks * MOE_BLOCK,), n_tok, jnp.int32).at[dest].set(flat_tok[order])
    buf_gate = jnp.zeros((n_blocks * MOE_BLOCK,), jnp.float32).at[dest].set(flat_g[order])
    block_e = jnp.minimum(jnp.searchsorted(pad_end, jnp.arange(n_blocks) * MOE_BLOCK, side='right'),
                          N_EXPERTS - 1)
    x_pad = jnp.concatenate([xt, jnp.zeros((1, D_MODEL), xt.dtype)], axis=0)
    xb = x_pad[buf_tok].reshape(n_blocks, MOE_BLOCK, D_MODEL)

    def expert_block(args):
        xblk, e = args
        h = xblk @ w_gu[e] + b_gu[e]
        gate = jnp.minimum(h[:, :D_FF], SWIGLU_LIMIT)
        up = jnp.clip(h[:, D_FF:], -SWIGLU_LIMIT, SWIGLU_LIMIT)
        act = (up + 1.0) * (gate * jax.nn.sigmoid(SWIGLU_ALPHA * gate))
        return act @ w_down[e] + b_down[e]

    yb = lax.map(expert_block, (xb, block_e)).reshape(-1, D_MODEL)
    y = jnp.zeros((n_tok + 1, D_MODEL), xt.dtype).at[buf_tok].add(yb * buf_gate[:, None].astype(yb.dtype))
    return y[:n_tok].reshape(shp)


def setup_inputs(seed: int = 0) -> dict:
    key = jax.random.key(seed)
    ks = jax.random.split(key, 32)
    f32 = jnp.float32
    n_pages = PAST_LEN // PAGE_SIZE
    n_used = DEC_BATCH * n_pages
    n_phys = n_used + n_used // 4
    win_buf = min(WINDOW, PAST_LEN)

    def nrm(k, shape, scale):
        return jax.random.normal(k, shape, f32) * scale

    inp = {}
    inp['x_prompt'] = nrm(ks[0], (BATCH, SEQ, D_MODEL), 1.0)
    inp['x_sample'] = nrm(ks[1], (DEC_BATCH, DEC_SEQ, D_MODEL), 1.0)
    inp['cache_k_diff'] = nrm(ks[2], (N_DIFF_LAYERS, n_phys, PAGE_SIZE, DIFF_KV_HEADS, 2, DIFF_HEAD_DIM), 1.0)
    inp['cache_v_diff'] = nrm(ks[3], (N_DIFF_LAYERS, n_phys, PAGE_SIZE, DIFF_KV_HEADS, 2 * DIFF_HEAD_DIM), 1.0)
    inp['cache_k_swa'] = nrm(ks[4], (N_SWA_LAYERS, DEC_BATCH, win_buf, SWA_KV_HEADS, SWA_HEAD_DIM), 1.0)
    inp['cache_v_swa'] = nrm(ks[5], (N_SWA_LAYERS, DEC_BATCH, win_buf, SWA_KV_HEADS, SWA_HEAD_DIM), 1.0)
    inp['page_table'] = jax.random.permutation(ks[6], n_phys)[:n_used].reshape(DEC_BATCH, n_pages).astype(jnp.int32)
    inp['ln1_g'] = 1.0 + nrm(ks[7], (DEPTH, D_MODEL), 0.02)
    inp['ln1_b'] = nrm(ks[8], (DEPTH, D_MODEL), 0.02)
    inp['ln2_g'] = 1.0 + nrm(ks[9], (DEPTH, D_MODEL), 0.02)
    inp['ln2_b'] = nrm(ks[10], (DEPTH, D_MODEL), 0.02)
    inp['w_qkv_diff'] = nrm(ks[11], (N_DIFF_LAYERS, D_MODEL, DIFF_QKV_DIM), D_MODEL ** -0.5)
    inp['diff_lam'] = nrm(ks[12], (N_DIFF_LAYERS, 4, DIFF_HEAD_DIM), 0.1)
    inp['diff_subln_g'] = 1.0 + nrm(ks[13], (N_DIFF_LAYERS, 2 * DIFF_HEAD_DIM), 0.02)
    inp['w_o_diff'] = nrm(ks[14], (N_DIFF_LAYERS, DIFF_O_DIM, D_MODEL), BETA * DIFF_O_DIM ** -0.5)
    inp['w_qkv_swa'] = nrm(ks[15], (N_SWA_LAYERS, D_MODEL, SWA_QKV_DIM), D_MODEL ** -0.5)
    inp['b_qkv_swa'] = nrm(ks[16], (N_SWA_LAYERS, SWA_QKV_DIM), 0.02)
    inp['swa_sinks'] = nrm(ks[17], (N_SWA_LAYERS, SWA_HEADS), 1.0)
    inp['w_o_swa'] = nrm(ks[18], (N_SWA_LAYERS, SWA_Q_DIM, D_MODEL), BETA * SWA_Q_DIM ** -0.5)
    inp['b_o_swa'] = nrm(ks[19], (N_SWA_LAYERS, D_MODEL), 0.02)
    inp['w_router'] = nrm(ks[20], (DEPTH, D_MODEL, N_EXPERTS), D_MODEL ** -0.5)
    inp['b_router'] = nrm(ks[21], (DEPTH, N_EXPERTS), 0.01)
    inp['w_gu'] = nrm(ks[22], (DEPTH, N_EXPERTS, D_MODEL, 2 * D_FF), D_MODEL ** -0.5)
    inp['b_gu'] = nrm(ks[23], (DEPTH, N_EXPERTS, 2 * D_FF), 0.02)
    inp['w_down'] = nrm(ks[24], (DEPTH, N_EXPERTS, D_FF, D_MODEL), BETA * D_FF ** -0.5)
    inp['b_down'] = nrm(ks[25], (DEPTH, N_EXPERTS, D_MODEL), 0.02)
    return inp


def reference(x_prompt, x_sample, cache_k_diff, cache_v_diff, cache_k_swa, cache_v_swa, page_table,
              ln1_g, ln1_b, ln2_g, ln2_b, w_qkv_diff, diff_lam, diff_subln_g, w_o_diff,
              w_qkv_swa, b_qkv_swa, swa_sinks, w_o_swa, b_o_swa,
              w_router, b_router, w_gu, b_gu, w_down, b_down):
    xp, xs = x_prompt, x_sample
    kdp, vdp, kds, vds = [], [], [], []
    ksp, vsp, kss, vss = [], [], [], []
    for i in range(DEPTH):
        j = i // N_MIXERS
        if i % N_MIXERS == 0:
            mp, kp, vp = diff_prompt(xp, w_qkv_diff[j], diff_lam[j], diff_subln_g[j], w_o_diff[j], i)
            ms, kn, vn = diff_sample(xs, cache_k_diff, cache_v_diff, j, page_table,
                                     w_qkv_diff[j], diff_lam[j], diff_subln_g[j], w_o_diff[j], i)
            kdp.append(kp); vdp.append(vp); kds.append(kn); vds.append(vn)
        else:
            mp, kp, vp = swa_prompt(xp, w_qkv_swa[j], b_qkv_swa[j], swa_sinks[j], w_o_swa[j], b_o_swa[j])
            ms, kn, vn = swa_sample(xs, cache_k_swa[j], cache_v_swa[j], w_qkv_swa[j], b_qkv_swa[j],
                                    swa_sinks[j], w_o_swa[j], b_o_swa[j])
            ksp.append(kp); vsp.append(vp); kss.append(kn); vss.append(vn)
        xp = layer_norm(ALPHA * xp + mp, ln1_g[i], ln1_b[i])
        xs = layer_norm(ALPHA * xs + ms, ln1_g[i], ln1_b[i])
        fp = moe(xp, w_router[i], b_router[i], w_gu[i], b_gu[i], w_down[i], b_down[i])
        fs = moe(xs, w_router[i], b_router[i], w_gu[i], b_gu[i], w_down[i], b_down[i])
        xp = layer_norm(ALPHA * xp + fp, ln2_g[i], ln2_b[i])
        xs = layer_norm(ALPHA * xs + fs, ln2_g[i], ln2_b[i])
    k_diff_prompt = jnp.stack(kdp)
    v_diff_prompt = jnp.stack(vdp)
    k_diff_sample = jnp.stack(kds)
    v_diff_sample = jnp.stack(vds)
    k_swa_prompt = jnp.stack(ksp)
    v_swa_prompt = jnp.stack(vsp)
    k_swa_sample = jnp.stack(kss)
    v_swa_sample = jnp.stack(vss)
    return (xp, xs, k_diff_prompt, v_diff_prompt, k_diff_sample, v_diff_sample,
            k_swa_prompt, v_swa_prompt, k_swa_sample, v_swa_sample)
```

```python
import functools
import math

import jax
import jax.numpy as jnp
import numpy as np
from jax import lax
from jax.experimental import pallas as pl
from jax.experimental.pallas import tpu as pltpu

F32 = jnp.float32
BF16 = jnp.bfloat16

D_MODEL = 1024
DEPTH = 4
PAGE_SIZE = 128
DIFF_HEAD_DIM = 64
DIFF_HEADS = 8
DIFF_KV_HEADS = 2
DIFF_GROUP = 4
DIFF_Q_DIM = 1024
DIFF_K_DIM = 256
DIFF_V_DIM = 256
SWA_HEAD_DIM = 64
SWA_HEADS = 16
SWA_KV_HEADS = 2
SWA_GROUP = 8
SWA_Q_DIM = 1024
SWA_KV_DIM = 128
WINDOW = 128
N_EXPERTS = 32
TOP_K = 4
D_FF = 1024
SWIGLU_ALPHA = 1.702
SWIGLU_LIMIT = 7.0
ALPHA = (2 * DEPTH) ** 0.25
LN_EPS = 1e-5
NEG_INF = -1e30

ROW_TILE = 384
DIFF_TQ = 256
MOE_TM = 512
DECODE_PAGES = 8
SWA_SEQS = 8
VMEM_LIMIT = 56 * 1024 * 1024


def _lambda_init(layer):
    return 0.8 - 0.6 * math.exp(-0.3 * layer)


def _cparams(sem):
    return pltpu.CompilerParams(dimension_semantics=sem, vmem_limit_bytes=VMEM_LIMIT)


def _linear_kernel(x_ref, w_ref, b_ref, o_ref, wbf_ref):
    @pl.when(pl.program_id(1) == 0)
    def _():
        wbf_ref[...] = w_ref[...].astype(BF16)

    acc = jnp.dot(x_ref[...].astype(BF16), wbf_ref[...], preferred_element_type=F32)
    o_ref[...] = acc + b_ref[...]


def linear(x, w_all, layer, bias, tn):
    t, k = x.shape
    n = w_all.shape[-1]
    return pl.pallas_call(
        _linear_kernel,
        out_shape=jax.ShapeDtypeStruct((t, n), F32),
        grid=(n // tn, t // ROW_TILE),
        in_specs=[
            pl.BlockSpec((ROW_TILE, k), lambda j, i: (i, 0)),
            pl.BlockSpec((None, k, tn), lambda j, i: (layer, 0, j)),
            pl.BlockSpec((1, tn), lambda j, i: (0, j)),
        ],
        out_specs=pl.BlockSpec((ROW_TILE, tn), lambda j, i: (i, j)),
        scratch_shapes=[pltpu.VMEM((k, tn), BF16)],
        compiler_params=_cparams(("arbitrary", "arbitrary")),
        name="linear",
    )(x, w_all, bias)


def _layer_norm_rows(z, g, b):
    mu = jnp.mean(z, axis=-1, keepdims=True)
    zc = z - mu
    var = jnp.mean(zc * zc, axis=-1, keepdims=True)
    return zc * lax.rsqrt(var + LN_EPS) * g + b


def _oproj_ln_kernel(a_ref, w_ref, bo_ref, x_ref, g_ref, b_ref, o_ref, wbf_ref):
    @pl.when(pl.program_id(0) == 0)
    def _():
        wbf_ref[...] = w_ref[...].astype(BF16)

    y = jnp.dot(a_ref[...], wbf_ref[...], preferred_element_type=F32) + bo_ref[...]
    o_ref[...] = _layer_norm_rows(ALPHA * x_ref[...] + y, g_ref[...], b_ref[...])


def oproj_ln(a, w_all, layer_slot, b_o, x, g_all, b_all, layer):
    t, d = x.shape
    row = lambda i: (i, 0)
    return pl.pallas_call(
        _oproj_ln_kernel,
        out_shape=jax.ShapeDtypeStruct((t, d), F32),
        grid=(t // ROW_TILE,),
        in_specs=[
            pl.BlockSpec((ROW_TILE, d), row),
            pl.BlockSpec((None, d, d), lambda i: (layer_slot, 0, 0)),
            pl.BlockSpec((1, d), lambda i: (0, 0)),
            pl.BlockSpec((ROW_TILE, d), row),
            pl.BlockSpec((None, 1, d), lambda i: (layer, 0, 0)),
            pl.BlockSpec((None, 1, d), lambda i: (layer, 0, 0)),
        ],
        out_specs=pl.BlockSpec((ROW_TILE, d), row),
        scratch_shapes=[pltpu.VMEM((d, d), BF16)],
        compiler_params=_cparams(("arbitrary",)),
        name="oproj_ln",
    )(a, w_all, b_o, x, g_all.reshape(-1, 1, d), b_all.reshape(-1, 1, d))


def _diff_prompt_kernel(lam_ref, sg_ref, q_ref, k_ref, v_ref, o_ref,
                        qs_ref, m_ref, l_ref, acc_ref, *, layer):
    tq = DIFF_TQ
    dh = DIFF_HEAD_DIM
    kvh = pl.program_id(1)
    qi = pl.program_id(2)

    for c in range(2):
        for g in range(DIFF_GROUP):
            col = (g * 2 + c) * dh
            qs_ref[c, g * tq:(g + 1) * tq, :] = (q_ref[:, col:col + dh] * (dh ** -0.5)).astype(BF16)
    m_ref[...] = jnp.full(m_ref.shape, NEG_INF, F32)
    l_ref[...] = jnp.zeros(l_ref.shape, F32)
    acc_ref[...] = jnp.zeros(acc_ref.shape, F32)

    row = lax.broadcasted_iota(jnp.int32, (tq, tq), 0)
    colm = lax.broadcasted_iota(jnp.int32, (tq, tq), 1)
    rel0 = (colm - row).astype(F32)

    def block(j, masked):
        start = pl.multiple_of(j * tq, tq)
        kb = k_ref[pl.ds(start, tq), :].astype(BF16)
        vb = v_ref[pl.ds(start, tq), :].astype(BF16)
        rel = rel0 + ((j - qi) * tq).astype(F32)
        s_all = [lax.dot_general(qs_ref[c], kb[:, c * dh:(c + 1) * dh],
                                 (((1,), (1,)), ((), ())), preferred_element_type=F32)
                 for c in range(2)]
        for g in range(DIFF_GROUP):
            slope = jnp.where(kvh == 0, 2.0 ** -(g + 1), 2.0 ** -(DIFF_GROUP + g + 1)).astype(F32)
            bias = slope * rel
            for c in range(2):
                idx = g * 2 + c
                s = s_all[c][g * tq:(g + 1) * tq, :] + bias
                if masked:
                    s = jnp.where(rel0 <= 0, s, NEG_INF)
                m_old = m_ref[idx]
                m_new = jnp.maximum(m_old, jnp.max(s, axis=-1, keepdims=True))
                alpha = jnp.exp(m_old - m_new)
                p = jnp.exp(s - m_new)
                l_ref[idx] = alpha * l_ref[idx] + jnp.sum(p, axis=-1, keepdims=True)
                acc_ref[idx] = alpha * acc_ref[idx] + jnp.dot(p.astype(BF16), vb,
                                                              preferred_element_type=F32)
                m_ref[idx] = m_new

    def body(j, carry):
        block(j, False)
        return carry

    lax.fori_loop(0, qi, body, 0)
    block(qi, True)

    lp = lam_ref[...]
    lam = (jnp.exp(jnp.sum(lp[0:1] * lp[1:2], axis=-1, keepdims=True))
           - jnp.exp(jnp.sum(lp[2:3] * lp[3:4], axis=-1, keepdims=True)) + _lambda_init(layer))
    gain = sg_ref[...] * (1.0 - _lambda_init(layer))
    for g in range(DIFF_GROUP):
        o1 = acc_ref[g * 2] / l_ref[g * 2]
        o2 = acc_ref[g * 2 + 1] / l_ref[g * 2 + 1]
        o = o1 - lam * o2
        o = o * lax.rsqrt(jnp.mean(o * o, axis=-1, keepdims=True) + LN_EPS) * gain
        o_ref[:, g * 2 * dh:(g + 1) * 2 * dh] = o.astype(o_ref.dtype)


def diff_prompt_attention(h, diff_lam, subln_g, slot, layer, n_batch, seq):
    tq = DIFF_TQ
    nq = seq // tq
    qw = DIFF_GROUP * 2 * DIFF_HEAD_DIM
    kw = 2 * DIFF_HEAD_DIM
    return pl.pallas_call(
        functools.partial(_diff_prompt_kernel, layer=layer),
        out_shape=jax.ShapeDtypeStruct((n_batch * seq, DIFF_Q_DIM), BF16),
        grid=(n_batch, DIFF_KV_HEADS, nq),
        in_specs=[
            pl.BlockSpec((None, 4, DIFF_HEAD_DIM), lambda b, h_, i: (slot, 0, 0)),
            pl.BlockSpec((None, 1, kw), lambda b, h_, i: (slot, 0, 0)),
            pl.BlockSpec((tq, qw), lambda b, h_, i: (b * nq + i, h_)),
            pl.BlockSpec((seq, kw), lambda b, h_, i: (b, DIFF_Q_DIM // kw + h_)),
            pl.BlockSpec((seq, kw), lambda b, h_, i: (b, (DIFF_Q_DIM + DIFF_K_DIM) // kw + h_)),
        ],
        out_specs=pl.BlockSpec((tq, qw), lambda b, h_, i: (b * nq + i, h_)),
        scratch_shapes=[
            pltpu.VMEM((2, DIFF_GROUP * tq, DIFF_HEAD_DIM), BF16),
            pltpu.VMEM((2 * DIFF_GROUP, tq, 1), F32),
            pltpu.VMEM((2 * DIFF_GROUP, tq, 1), F32),
            pltpu.VMEM((2 * DIFF_GROUP, tq, 2 * DIFF_HEAD_DIM), F32),
        ],
        compiler_params=_cparams(("arbitrary", "arbitrary", "arbitrary")),
        name="diff_prompt_attn",
    )(diff_lam, subln_g.reshape(-1, 1, kw), h, h, h)


def _diff_decode_kernel(pt_ref, lam_ref, sg_ref, slope_ref, wq_ref, kn_ref, vn_ref, *rest,
                        layer, n_pages, past_len):
    del pt_ref
    np_ = DECODE_PAGES
    k_refs = rest[:np_]
    v_refs = rest[np_:2 * np_]
    o_ref = rest[2 * np_]
    m_ref, l_ref, acc_ref = rest[2 * np_ + 1:]
    jc = pl.program_id(1)
    n_chunks = n_pages // np_
    dv = 2 * DIFF_HEAD_DIM

    @pl.when(jc == 0)
    def _():
        m_ref[...] = jnp.full(m_ref.shape, NEG_INF, F32)
        l_ref[...] = jnp.zeros(l_ref.shape, F32)
        acc_ref[...] = jnp.zeros(acc_ref.shape, F32)

    wq = wq_ref[...]
    slope = slope_ref[...]
    lane = lax.broadcasted_iota(jnp.int32, (1, PAGE_SIZE), 1)
    rowi = lax.broadcasted_iota(jnp.int32, (16, 1), 0)
    is_kv0 = (rowi % (2 * DIFF_GROUP)) < DIFF_GROUP

    s_pages = []
    for g in range(np_):
        kt = k_refs[g][...].astype(BF16)
        s = jnp.dot(wq, kt, preferred_element_type=F32)
        k_pos = (jc * np_ + g) * PAGE_SIZE + lane
        s_pages.append(s - slope * (past_len - k_pos).astype(F32))
    m_old = m_ref[...]
    m_new = m_old
    for s in s_pages:
        m_new = jnp.maximum(m_new, jnp.max(s, axis=-1, keepdims=True))
    alpha = jnp.exp(m_old - m_new)
    l_new = alpha * l_ref[...]
    acc = alpha * acc_ref[...]
    for g in range(np_):
        p = jnp.exp(s_pages[g] - m_new)
        l_new = l_new + jnp.sum(p, axis=-1, keepdims=True)
        pb = p.astype(BF16)
        v = v_refs[g][...].astype(BF16)
        pv = jnp.dot(pb, v, preferred_element_type=F32)
        acc = acc + jnp.where(is_kv0, pv[:, :dv], pv[:, dv:])
    m_ref[...] = m_new
    l_ref[...] = l_new
    acc_ref[...] = acc

    @pl.when(jc == n_chunks - 1)
    def _():
        kn = kn_ref[...]
        vn = vn_ref[...]
        s_n = jnp.sum(wq.astype(F32) * kn, axis=-1, keepdims=True)
        m_o = m_ref[...]
        m_f = jnp.maximum(m_o, s_n)
        a_f = jnp.exp(m_o - m_f)
        p_n = jnp.exp(s_n - m_f)
        l_f = a_f * l_ref[...] + p_n
        v_rows = jnp.where(is_kv0, vn[:, :dv], vn[:, dv:])
        acc_f = a_f * acc_ref[...] + p_n * v_rows
        o_maps = acc_f / l_f
        lp = lam_ref[...]
        lam = (jnp.exp(jnp.sum(lp[0:1] * lp[1:2], axis=-1, keepdims=True))
               - jnp.exp(jnp.sum(lp[2:3] * lp[3:4], axis=-1, keepdims=True)) + _lambda_init(layer))
        o = o_maps[:DIFF_HEADS] - lam * o_maps[DIFF_HEADS:]
        gain = sg_ref[...] * (1.0 - _lambda_init(layer))
        o = o * lax.rsqrt(jnp.mean(o * o, axis=-1, keepdims=True) + LN_EPS) * gain
        o_ref[...] = o.astype(o_ref.dtype)


def diff_decode_attention(hs, kt_cache, v_cache, page_table, diff_lam, subln_g, slot, layer):
    n_seq = hs.shape[0]
    n_pages = page_table.shape[1]
    past_len = n_pages * PAGE_SIZE
    dh = DIFF_HEAD_DIM
    q = hs[:, :DIFF_Q_DIM].reshape(n_seq, DIFF_KV_HEADS, DIFF_GROUP, 2, dh) * (dh ** -0.5)
    eye = jnp.eye(2, dtype=F32)
    wq = jnp.einsum('bkgcd,kK,cC->bckgKCd', q, eye, eye).reshape(n_seq, 16, DIFF_K_DIM).astype(BF16)
    kn = hs[:, DIFF_Q_DIM:DIFF_Q_DIM + DIFF_K_DIM].reshape(n_seq, 1, DIFF_K_DIM)
    vn = hs[:, DIFF_Q_DIM + DIFF_K_DIM:].reshape(n_seq, 1, DIFF_V_DIM)
    heads = np.arange(DIFF_HEADS, dtype=np.float32).reshape(DIFF_KV_HEADS, DIFF_GROUP)
    slopes = np.tile((2.0 ** -(heads + 1.0)).reshape(1, DIFF_HEADS), (2, 1)).reshape(16, 1)
    slopes = jnp.asarray(slopes, F32)

    np_ = DECODE_PAGES
    n_chunks = n_pages // np_

    def page_map(g):
        return lambda b, jc, pt: (slot, pt[b, jc * np_ + g], 0, 0)

    in_specs = [
        pl.BlockSpec((None, 4, dh), lambda b, jc, pt: (slot, 0, 0)),
        pl.BlockSpec((None, 1, 2 * dh), lambda b, jc, pt: (slot, 0, 0)),
        pl.BlockSpec((16, 1), lambda b, jc, pt: (0, 0)),
        pl.BlockSpec((None, 16, DIFF_K_DIM), lambda b, jc, pt: (b, 0, 0)),
        pl.BlockSpec((None, 1, DIFF_K_DIM), lambda b, jc, pt: (b, 0, 0)),
        pl.BlockSpec((None, 1, DIFF_V_DIM), lambda b, jc, pt: (b, 0, 0)),
    ]
    in_specs += [pl.BlockSpec((None, None, DIFF_K_DIM, PAGE_SIZE), page_map(g)) for g in range(np_)]
    in_specs += [pl.BlockSpec((None, None, PAGE_SIZE, DIFF_V_DIM), page_map(g)) for g in range(np_)]
    out = pl.pallas_call(
        functools.partial(_diff_decode_kernel, layer=layer, n_pages=n_pages, past_len=past_len),
        out_shape=jax.ShapeDtypeStruct((n_seq, DIFF_HEADS, 2 * dh), BF16),
        grid_spec=pltpu.PrefetchScalarGridSpec(
            num_scalar_prefetch=1,
            grid=(n_seq, n_chunks),
            in_specs=in_specs,
            out_specs=pl.BlockSpec((None, DIFF_HEADS, 2 * dh), lambda b, jc, pt: (b, 0, 0)),
            scratch_shapes=[
                pltpu.VMEM((16, 1), F32),
                pltpu.VMEM((16, 1), F32),
                pltpu.VMEM((16, 2 * dh), F32),
            ],
        ),
        compiler_params=_cparams(("arbitrary", "arbitrary")),
        name="diff_decode_attn",
    )(page_table, diff_lam, subln_g.reshape(-1, 1, 2 * dh), slopes, wq, kn, vn,
      *([kt_cache] * np_), *([v_cache] * np_))
    return out.reshape(n_seq, DIFF_Q_DIM)


def _swa_slopes():
    return 2.0 ** (-8.0 * np.arange(1, SWA_HEADS + 1, dtype=np.float32) / SWA_HEADS)


def _swa_prompt_kernel(sink_ref, slope_ref, q_ref, cur_ref, prev_ref, o_ref, *, n_blocks):
    w = WINDOW
    dh = SWA_HEAD_DIM
    bi = pl.program_id(0) % n_blocks
    row = lax.broadcasted_iota(jnp.int32, (w, w), 0)
    col = lax.broadcasted_iota(jnp.int32, (w, w), 1)
    dist_cur = (row - col).astype(F32)
    dist_prev = dist_cur + float(w)
    ok_cur = col <= row
    ok_prev = jnp.logical_and(col > row, (jnp.zeros_like(col) + bi) > 0)
    outs = []
    for kh in range(SWA_KV_HEADS):
        kc = cur_ref[:, kh * dh:(kh + 1) * dh].astype(BF16)
        kp = prev_ref[:, kh * dh:(kh + 1) * dh].astype(BF16)
        vc = cur_ref[:, SWA_KV_DIM + kh * dh:SWA_KV_DIM + (kh + 1) * dh].astype(BF16)
        vp = prev_ref[:, SWA_KV_DIM + kh * dh:SWA_KV_DIM + (kh + 1) * dh].astype(BF16)
        for g in range(SWA_GROUP):
            hd = kh * SWA_GROUP + g
            q = (q_ref[:, hd * dh:(hd + 1) * dh] * (dh ** -0.5)).astype(BF16)
            dims = (((1,), (1,)), ((), ()))
            s_c = lax.dot_general(q, kc, dims, preferred_element_type=F32)
            s_p = lax.dot_general(q, kp, dims, preferred_element_type=F32)
            slope = slope_ref[hd]
            sink = sink_ref[hd]
            s_c = jnp.where(ok_cur, s_c - slope * dist_cur, NEG_INF)
            s_p = jnp.where(ok_prev, s_p - slope * dist_prev, NEG_INF)
            m = jnp.maximum(jnp.maximum(jnp.max(s_c, axis=-1, keepdims=True),
                                        jnp.max(s_p, axis=-1, keepdims=True)), sink)
            e_c = jnp.exp(s_c - m)
            e_p = jnp.exp(s_p - m)
            den = (jnp.sum(e_c, axis=-1, keepdims=True) + jnp.sum(e_p, axis=-1, keepdims=True)
                   + jnp.exp(sink - m))
            inv = 1.0 / den
            pv = (jnp.dot((e_c * inv).astype(BF16), vc, preferred_element_type=F32)
                  + jnp.dot((e_p * inv).astype(BF16), vp, preferred_element_type=F32))
            outs.append(pv)
    o_ref[...] = jnp.concatenate(outs, axis=-1).astype(o_ref.dtype)


def swa_prompt_attention(h, sinks, n_batch, seq):
    nb = seq // WINDOW
    kvw = 2 * SWA_KV_DIM
    kv_col = SWA_Q_DIM // kvw
    slopes = jnp.asarray(_swa_slopes(), F32)
    smem = pl.BlockSpec(memory_space=pltpu.SMEM)
    return pl.pallas_call(
        functools.partial(_swa_prompt_kernel, n_blocks=nb),
        out_shape=jax.ShapeDtypeStruct((n_batch * seq, SWA_Q_DIM), BF16),
        grid=(n_batch * nb,),
        in_specs=[
            smem, smem,
            pl.BlockSpec((WINDOW, SWA_Q_DIM), lambda r: (r, 0)),
            pl.BlockSpec((WINDOW, kvw), lambda r: (r, kv_col)),
            pl.BlockSpec((WINDOW, kvw), lambda r: (jnp.maximum(r - 1, 0), kv_col)),
        ],
        out_specs=pl.BlockSpec((WINDOW, SWA_Q_DIM), lambda r: (r, 0)),
        compiler_params=_cparams(("arbitrary",)),
        name="swa_prompt_attn",
    )(sinks, slopes, h, h, h)


def _swa_decode_kernel(sink_ref, slope_ref, wq_ref, kn_ref, vn_ref, kt_ref, vt_ref, o_ref, *, win_buf):
    dh = SWA_HEAD_DIM
    lane = lax.broadcasted_iota(jnp.int32, (1, win_buf), 1)
    dist = (win_buf - lane).astype(F32)
    ok = dist < float(WINDOW)
    rowi = lax.broadcasted_iota(jnp.int32, (SWA_HEADS, 1), 0)
    is_kv0 = rowi < SWA_GROUP
    sink = sink_ref[...]
    slope = slope_ref[...]
    for sq in range(SWA_SEQS):
        wq = wq_ref[sq]
        kt = kt_ref[sq].astype(BF16)
        vt = vt_ref[sq].astype(BF16)
        s = jnp.dot(wq, kt, preferred_element_type=F32)
        s = jnp.where(ok, s - slope * dist, NEG_INF)
        s_n = jnp.sum(wq.astype(F32) * kn_ref[sq], axis=-1, keepdims=True)
        m = jnp.maximum(jnp.maximum(jnp.max(s, axis=-1, keepdims=True), s_n), sink)
        e = jnp.exp(s - m)
        e_n = jnp.exp(s_n - m)
        den = jnp.sum(e, axis=-1, keepdims=True) + e_n + jnp.exp(sink - m)
        inv = 1.0 / den
        pv = lax.dot_general((e * inv).astype(BF16), vt, (((1,), (1,)), ((), ())),
                             preferred_element_type=F32)
        pv = pv + (e_n * inv) * vn_ref[sq]
        o_ref[sq] = jnp.where(is_kv0, pv[:, :dh], pv[:, dh:]).astype(o_ref.dtype)


def swa_decode_attention(hs, kt_buf, vt_buf, sinks):
    n_seq = hs.shape[0]
    win_buf = kt_buf.shape[-1]
    dh = SWA_HEAD_DIM
    q = hs[:, :SWA_Q_DIM].reshape(n_seq, SWA_KV_HEADS, SWA_GROUP, dh) * (dh ** -0.5)
    eye = jnp.eye(SWA_KV_HEADS, dtype=F32)
    wq = jnp.einsum('bkgd,kK->bkgKd', q, eye).reshape(n_seq, SWA_HEADS, SWA_KV_DIM).astype(BF16)
    kn = hs[:, SWA_Q_DIM:SWA_Q_DIM + SWA_KV_DIM].reshape(n_seq, 1, SWA_KV_DIM)
    vn = hs[:, SWA_Q_DIM + SWA_KV_DIM:].reshape(n_seq, 1, SWA_KV_DIM)
    slopes = jnp.asarray(_swa_slopes().reshape(SWA_HEADS, 1), F32)
    sb = SWA_SEQS
    seq3 = lambda i: (i, 0, 0)
    out = pl.pallas_call(
        functools.partial(_swa_decode_kernel, win_buf=win_buf),
        out_shape=jax.ShapeDtypeStruct((n_seq, SWA_HEADS, dh), BF16),
        grid=(n_seq // sb,),
        in_specs=[
            pl.BlockSpec((SWA_HEADS, 1), lambda i: (0, 0)),
            pl.BlockSpec((SWA_HEADS, 1), lambda i: (0, 0)),
            pl.BlockSpec((sb, SWA_HEADS, SWA_KV_DIM), seq3),
            pl.BlockSpec((sb, 1, SWA_KV_DIM), seq3),
            pl.BlockSpec((sb, 1, SWA_KV_DIM), seq3),
            pl.BlockSpec((sb, SWA_KV_DIM, win_buf), seq3),
            pl.BlockSpec((sb, SWA_KV_DIM, win_buf), seq3),
        ],
        out_specs=pl.BlockSpec((sb, SWA_HEADS, dh), seq3),
        compiler_params=_cparams(("arbitrary",)),
        name="swa_decode_attn",
    )(sinks.reshape(SWA_HEADS, 1), slopes, wq, kn, vn, kt_buf, vt_buf)
    return out.reshape(n_seq, SWA_Q_DIM)


def _router_kernel(x_ref, w_ref, b_ref, e_ref, g_ref, r_ref, cnt_ref, carry_ref):
    tm = x_ref.shape[0]

    @pl.when(pl.program_id(0) == 0)
    def _():
        carry_ref[...] = jnp.zeros(carry_ref.shape, F32)

    logits = jnp.dot(x_ref[...].astype(BF16), w_ref[...].astype(BF16),
                     preferred_element_type=F32) + b_ref[...]
    lane = lax.broadcasted_iota(jnp.int32, (tm, N_EXPERTS), 1).astype(F32)
    work = logits
    vals, idxs = [], []
    for _ in range(TOP_K):
        mx = jnp.max(work, axis=-1, keepdims=True)
        ix = jnp.min(jnp.where(work == mx, lane, float(N_EXPERTS)), axis=-1, keepdims=True)
        vals.append(mx)
        idxs.append(ix)
        work = jnp.where(lane == ix, -jnp.inf, work)
    ex = [jnp.exp(v - vals[0]) for v in vals]
    den = ex[0] + ex[1] + ex[2] + ex[3]

    onehot = jnp.zeros((tm, N_EXPERTS), F32)
    for ix in idxs:
        onehot = onehot + (lane == ix).astype(F32)
    ri = lax.broadcasted_iota(jnp.int32, (tm, tm), 0)
    ci = lax.broadcasted_iota(jnp.int32, (tm, tm), 1)
    tri = (ci < ri).astype(BF16)
    before = jnp.dot(tri, onehot.astype(BF16), preferred_element_type=F32) + carry_ref[...]

    col4 = lax.broadcasted_iota(jnp.int32, (tm, TOP_K), 1)
    e_out = jnp.zeros((tm, TOP_K), jnp.int32)
    g_out = jnp.zeros((tm, TOP_K), F32)
    r_out = jnp.zeros((tm, TOP_K), jnp.int32)
    for k in range(TOP_K):
        rank = jnp.sum(jnp.where(lane == idxs[k], before, 0.0), axis=-1, keepdims=True)
        e_out = jnp.where(col4 == k, idxs[k].astype(jnp.int32), e_out)
        g_out = jnp.where(col4 == k, ex[k] / den, g_out)
        r_out = jnp.where(col4 == k, rank.astype(jnp.int32), r_out)
    e_ref[...] = e_out
    g_ref[...] = g_out
    r_ref[...] = r_out
    carry_ref[...] = carry_ref[...] + jnp.sum(onehot, axis=0, keepdims=True)
    cnt_ref[...] = carry_ref[...]


def router(x, w_router, b_router, layer):
    t, d = x.shape
    row = lambda i: (i, 0)
    return pl.pallas_call(
        _router_kernel,
        out_shape=(
            jax.ShapeDtypeStruct((t, TOP_K), jnp.int32),
            jax.ShapeDtypeStruct((t, TOP_K), F32),
            jax.ShapeDtypeStruct((t, TOP_K), jnp.int32),
            jax.ShapeDtypeStruct((1, N_EXPERTS), F32),
        ),
        grid=(t // ROW_TILE,),
        in_specs=[
            pl.BlockSpec((ROW_TILE, d), row),
            pl.BlockSpec((None, d, N_EXPERTS), lambda i: (layer, 0, 0)),
            pl.BlockSpec((None, 1, N_EXPERTS), lambda i: (layer, 0, 0)),
        ],
        out_specs=(
            pl.BlockSpec((ROW_TILE, TOP_K), row),
            pl.BlockSpec((ROW_TILE, TOP_K), row),
            pl.BlockSpec((ROW_TILE, TOP_K), row),
            pl.BlockSpec((1, N_EXPERTS), lambda i: (0, 0)),
        ),
        scratch_shapes=[pltpu.VMEM((1, N_EXPERTS), F32)],
        compiler_params=_cparams(("arbitrary",)),
        name="router",
    )(x, w_router, b_router.reshape(-1, 1, N_EXPERTS))


def _experts_kernel(be_ref, nu_ref, x_ref, wgu_ref, bgu_ref, wd_ref, bd_ref, o_ref, wgu_bf, wd_bf):
    s = pl.program_id(0)
    prev = be_ref[jnp.maximum(s - 1, 0)]
    new_expert = jnp.logical_or(s == 0, be_ref[s] != prev)
    used = s < nu_ref[0]

    @pl.when(jnp.logical_and(new_expert, used))
    def _():
        wgu_bf[...] = wgu_ref[...].astype(BF16)
        wd_bf[...] = wd_ref[...].astype(BF16)

    @pl.when(used)
    def _():
        h = jnp.dot(x_ref[...], wgu_bf[...], preferred_element_type=F32) + bgu_ref[...]
        gate = jnp.minimum(h[:, :D_FF], SWIGLU_LIMIT)
        up = jnp.clip(h[:, D_FF:], -SWIGLU_LIMIT, SWIGLU_LIMIT)
        act = (up + 1.0) * (gate * jax.nn.sigmoid(SWIGLU_ALPHA * gate))
        o_ref[...] = jnp.dot(act.astype(BF16), wd_bf[...], preferred_element_type=F32) + bd_ref[...]

    @pl.when(jnp.logical_not(used))
    def _():
        o_ref[...] = jnp.zeros(o_ref.shape, o_ref.dtype)


def experts(xs, block_e, n_used, w_gu, b_gu, w_down, b_down, layer):
    r, d = xs.shape
    n_blocks = r // MOE_TM
    return pl.pallas_call(
        _experts_kernel,
        out_shape=jax.ShapeDtypeStruct((r, d), F32),
        grid_spec=pltpu.PrefetchScalarGridSpec(
            num_scalar_prefetch=2,
            grid=(n_blocks,),
            in_specs=[
                pl.BlockSpec((MOE_TM, d), lambda s, be, nu: (s, 0)),
                pl.BlockSpec((None, None, d, 2 * D_FF), lambda s, be, nu: (layer, be[s], 0, 0)),
                pl.BlockSpec((None, None, 1, 2 * D_FF), lambda s, be, nu: (layer, be[s], 0, 0)),
                pl.BlockSpec((None, None, D_FF, d), lambda s, be, nu: (layer, be[s], 0, 0)),
                pl.BlockSpec((None, None, 1, d), lambda s, be, nu: (layer, be[s], 0, 0)),
            ],
            out_specs=pl.BlockSpec((MOE_TM, d), lambda s, be, nu: (s, 0)),
            scratch_shapes=[
                pltpu.VMEM((d, 2 * D_FF), BF16),
                pltpu.VMEM((D_FF, d), BF16),
            ],
        ),
        compiler_params=_cparams(("arbitrary",)),
        name="experts",
    )(block_e, n_used, xs, w_gu, b_gu.reshape(DEPTH, N_EXPERTS, 1, 2 * D_FF),
      w_down, b_down.reshape(DEPTH, N_EXPERTS, 1, d))


def _combine_ln_kernel(x_ref, y0_ref, y1_ref, y2_ref, y3_ref, gt_ref, g_ref, b_ref, o_ref):
    gt = gt_ref[...]
    f = y0_ref[...] * gt[:, 0:1]
    for k, y_ref in enumerate((y1_ref, y2_ref, y3_ref), start=1):
        f = f + y_ref[...] * gt[:, k:k + 1]
    o_ref[...] = _layer_norm_rows(ALPHA * x_ref[...] + f, g_ref[...], b_ref[...])


def combine_ln(x, ys, gates, g_all, b_all, layer):
    t, d = x.shape
    row = lambda i: (i, 0)
    return pl.pallas_call(
        _combine_ln_kernel,
        out_shape=jax.ShapeDtypeStruct((t, d), F32),
        grid=(t // ROW_TILE,),
        in_specs=[
            pl.BlockSpec((ROW_TILE, d), row),
            pl.BlockSpec((ROW_TILE, d), row),
            pl.BlockSpec((ROW_TILE, d), row),
            pl.BlockSpec((ROW_TILE, d), row),
            pl.BlockSpec((ROW_TILE, d), row),
            pl.BlockSpec((ROW_TILE, TOP_K), row),
            pl.BlockSpec((None, 1, d), lambda i: (layer, 0, 0)),
            pl.BlockSpec((None, 1, d), lambda i: (layer, 0, 0)),
        ],
        out_specs=pl.BlockSpec((ROW_TILE, d), row),
        compiler_params=_cparams(("arbitrary",)),
        name="combine_ln",
    )(x, *ys, gates, g_all.reshape(-1, 1, d), b_all.reshape(-1, 1, d))


def moe_layer(x, layer, w_router, b_router, w_gu, b_gu, w_down, b_down, ln_g, ln_b):
    t, d = x.shape
    top_e, gates, rank, counts = router(x, w_router, b_router, layer)
    counts = counts.reshape(N_EXPERTS).astype(jnp.int32)
    padded = (counts + MOE_TM - 1) // MOE_TM * MOE_TM
    pad_end = jnp.cumsum(padded)
    pad_start = pad_end - padded
    n_blocks = -(-t * TOP_K // MOE_TM) + N_EXPERTS
    dest = pad_start[top_e] + rank
    tok = jnp.arange(t * TOP_K, dtype=jnp.int32) // TOP_K
    buf_tok = jnp.zeros((n_blocks * MOE_TM,), jnp.int32).at[dest.reshape(-1)].set(tok)
    block_e = jnp.minimum(
        jnp.searchsorted(pad_end, jnp.arange(n_blocks, dtype=jnp.int32) * MOE_TM, side='right'),
        N_EXPERTS - 1).astype(jnp.int32)
    n_used = (pad_end[-1] // MOE_TM).astype(jnp.int32).reshape(1)
    xs = x.astype(BF16)[buf_tok]
    ys = experts(xs, block_e, n_used, w_gu, b_gu, w_down, b_down, layer)
    y_k = [ys[dest[:, k]] for k in range(TOP_K)]
    return combine_ln(x, y_k, gates, ln_g, ln_b, layer)


def kernel(x_prompt, x_sample, cache_k_diff, cache_v_diff, cache_k_swa, cache_v_swa, page_table,
           ln1_g, ln1_b, ln2_g, ln2_b, w_qkv_diff, diff_lam, diff_subln_g, w_o_diff,
           w_qkv_swa, b_qkv_swa, swa_sinks, w_o_swa, b_o_swa,
           w_router, b_router, w_gu, b_gu, w_down, b_down):
    n_batch, seq, d = x_prompt.shape
    n_seq = x_sample.shape[0]
    tp = n_batch * seq
    x = jnp.concatenate([x_prompt.reshape(tp, d), x_sample.reshape(n_seq, d)], axis=0)

    n_layers_diff, n_phys = cache_k_diff.shape[:2]
    kt_cache = jnp.transpose(cache_k_diff, (0, 1, 3, 4, 5, 2)).reshape(
        n_layers_diff, n_phys, DIFF_K_DIM, PAGE_SIZE)
    v_cache = cache_v_diff.reshape(n_layers_diff, n_phys, PAGE_SIZE, DIFF_V_DIM)
    win_buf = cache_k_swa.shape[2]
    kt_swa = jnp.transpose(cache_k_swa, (0, 1, 3, 4, 2)).reshape(-1, n_seq, SWA_KV_DIM, win_buf)
    vt_swa = jnp.transpose(cache_v_swa, (0, 1, 3, 4, 2)).reshape(-1, n_seq, SWA_KV_DIM, win_buf)

    zero_bias_qkv = jnp.zeros((1, w_qkv_diff.shape[-1]), F32)
    zero_bias_o = jnp.zeros((1, d), F32)

    kdp, vdp, kds, vds = [], [], [], []
    ksp, vsp, kss, vss = [], [], [], []
    for i in range(DEPTH):
        j = i // 2
        if i % 2 == 0:
            h = linear(x, w_qkv_diff, j, zero_bias_qkv, 512)
            hs = h[tp:]
            a_p = diff_prompt_attention(h, diff_lam, diff_subln_g, j, i, n_batch, seq)
            a_s = diff_decode_attention(hs, kt_cache, v_cache, page_table, diff_lam, diff_subln_g, j, i)
            kq, vq = DIFF_Q_DIM, DIFF_Q_DIM + DIFF_K_DIM
            kdp.append(h[:tp, kq:vq].reshape(n_batch, seq, DIFF_KV_HEADS, 2, DIFF_HEAD_DIM))
            vdp.append(h[:tp, vq:].reshape(n_batch, seq, DIFF_KV_HEADS, 2 * DIFF_HEAD_DIM))
            kds.append(hs[:, kq:vq].reshape(n_seq, 1, DIFF_KV_HEADS, 2, DIFF_HEAD_DIM))
            vds.append(hs[:, vq:].reshape(n_seq, 1, DIFF_KV_HEADS, 2 * DIFF_HEAD_DIM))
            a = jnp.concatenate([a_p, a_s], axis=0)
            x = oproj_ln(a, w_o_diff, j, zero_bias_o, x, ln1_g, ln1_b, i)
        else:
            h = linear(x, w_qkv_swa, j, b_qkv_swa[j].reshape(1, -1), 640)
            hs = h[tp:]
            a_p = swa_prompt_attention(h, swa_sinks[j], n_batch, seq)
            a_s = swa_decode_attention(hs, kt_swa[j], vt_swa[j], swa_sinks[j])
            kq, vq = SWA_Q_DIM, SWA_Q_DIM + SWA_KV_DIM
            keep = min(WINDOW, seq)
            hp = h[:tp].reshape(n_batch, seq, -1)[:, seq - keep:]
            ksp.append(hp[..., kq:vq].reshape(n_batch, keep, SWA_KV_HEADS, SWA_HEAD_DIM))
            vsp.append(hp[..., vq:].reshape(n_batch, keep, SWA_KV_HEADS, SWA_HEAD_DIM))
            k_new = hs[:, kq:vq].reshape(n_seq, 1, SWA_KV_HEADS, SWA_HEAD_DIM)
            v_new = hs[:, vq:].reshape(n_seq, 1, SWA_KV_HEADS, SWA_HEAD_DIM)
            kss.append(jnp.concatenate([cache_k_swa[j], k_new], axis=1)[:, -win_buf:])
            vss.append(jnp.concatenate([cache_v_swa[j], v_new], axis=1)[:, -win_buf:])
            a = jnp.concatenate([a_p, a_s], axis=0)
            x = oproj_ln(a, w_o_swa, j, b_o_swa[j].reshape(1, -1), x, ln1_g, ln1_b, i)
        x = moe_layer(x, i, w_router, b_router, w_gu, b_gu, w_down, b_down, ln2_g, ln2_b)

    return (x[:tp].reshape(n_batch, seq, d), x[tp:].reshape(n_seq, 1, d),
            jnp.stack(kdp), jnp.stack(vdp), jnp.stack(kds), jnp.stack(vds),
            jnp.stack(ksp), jnp.stack(vsp), jnp.stack(kss), jnp.stack(vss))
```

```python
import functools
import math

import jax
import jax.numpy as jnp
import numpy as np
from jax import lax
from jax.experimental import pallas as pl
from jax.experimental.pallas import tpu as pltpu

F32 = jnp.float32
BF16 = jnp.bfloat16

D_MODEL = 1024
DEPTH = 4
PAGE_SIZE = 128
DIFF_HEAD_DIM = 64
DIFF_HEADS = 8
DIFF_KV_HEADS = 2
DIFF_GROUP = 4
DIFF_Q_DIM = 1024
DIFF_K_DIM = 256
DIFF_V_DIM = 256
SWA_HEAD_DIM = 64
SWA_HEADS = 16
SWA_KV_HEADS = 2
SWA_GROUP = 8
SWA_Q_DIM = 1024
SWA_KV_DIM = 128
WINDOW = 128
N_EXPERTS = 32
TOP_K = 4
D_FF = 1024
SWIGLU_ALPHA = 1.702
SWIGLU_LIMIT = 7.0
ALPHA = (2 * DEPTH) ** 0.25
LN_EPS = 1e-5
NEG_INF = -1e30

ROW_TILE = 384
DIFF_TQ = 256
MOE_TM = 512
DECODE_PAGES = 16
SWA_SEQS = 8
VMEM_LIMIT = 56 * 1024 * 1024


def _lambda_init(layer):
    return 0.8 - 0.6 * math.exp(-0.3 * layer)


def _cparams(sem):
    return pltpu.CompilerParams(dimension_semantics=sem, vmem_limit_bytes=VMEM_LIMIT)


def _linear_kernel(x_ref, w_ref, b_ref, o_ref, wbf_ref):
    @pl.when(pl.program_id(1) == 0)
    def _():
        wbf_ref[...] = w_ref[...].astype(BF16)

    acc = jnp.dot(x_ref[...].astype(BF16), wbf_ref[...], preferred_element_type=F32)
    o_ref[...] = acc + b_ref[...]


def linear(x, w_all, layer, bias, tn):
    t, k = x.shape
    n = w_all.shape[-1]
    return pl.pallas_call(
        _linear_kernel,
        out_shape=jax.ShapeDtypeStruct((t, n), F32),
        grid=(n // tn, t // ROW_TILE),
        in_specs=[
            pl.BlockSpec((ROW_TILE, k), lambda j, i: (i, 0)),
            pl.BlockSpec((None, k, tn), lambda j, i: (layer, 0, j)),
            pl.BlockSpec((1, tn), lambda j, i: (0, j)),
        ],
        out_specs=pl.BlockSpec((ROW_TILE, tn), lambda j, i: (i, j)),
        scratch_shapes=[pltpu.VMEM((k, tn), BF16)],
        compiler_params=_cparams(("arbitrary", "arbitrary")),
        name="linear",
    )(x, w_all, bias)


def _layer_norm_rows(z, g, b):
    mu = jnp.mean(z, axis=-1, keepdims=True)
    zc = z - mu
    var = jnp.mean(zc * zc, axis=-1, keepdims=True)
    return zc * lax.rsqrt(var + LN_EPS) * g + b


TILE_ROWS = D_MODEL // 128


def _store_token_tiles(ref, val):
    n = val.shape[0]
    for c in range(TILE_ROWS):
        ref[pl.ds(c, n, stride=TILE_ROWS), :] = val[:, c * 128:(c + 1) * 128]


def _load_token_tiles(ref):
    n = ref.shape[0] // TILE_ROWS
    return jnp.concatenate([ref[pl.ds(c, n, stride=TILE_ROWS), :] for c in range(TILE_ROWS)], axis=-1)


def _oproj_ln_kernel(a_ref, w_ref, bo_ref, x_ref, g_ref, b_ref, o_ref, ot_ref, wbf_ref):
    @pl.when(pl.program_id(0) == 0)
    def _():
        wbf_ref[...] = w_ref[...].astype(BF16)

    y = jnp.dot(a_ref[...], wbf_ref[...], preferred_element_type=F32) + bo_ref[...]
    z = _layer_norm_rows(ALPHA * x_ref[...] + y, g_ref[...], b_ref[...])
    o_ref[...] = z
    _store_token_tiles(ot_ref, z)


def oproj_ln(a, w_all, layer_slot, b_o, x, g_all, b_all, layer):
    t, d = x.shape
    row = lambda i: (i, 0)
    return pl.pallas_call(
        _oproj_ln_kernel,
        out_shape=(jax.ShapeDtypeStruct((t, d), F32),
                   jax.ShapeDtypeStruct((t * TILE_ROWS, 128), F32)),
        grid=(t // ROW_TILE,),
        in_specs=[
            pl.BlockSpec((ROW_TILE, d), row),
            pl.BlockSpec((None, d, d), lambda i: (layer_slot, 0, 0)),
            pl.BlockSpec((1, d), lambda i: (0, 0)),
            pl.BlockSpec((ROW_TILE, d), row),
            pl.BlockSpec((None, 1, d), lambda i: (layer, 0, 0)),
            pl.BlockSpec((None, 1, d), lambda i: (layer, 0, 0)),
        ],
        out_specs=(pl.BlockSpec((ROW_TILE, d), row),
                   pl.BlockSpec((ROW_TILE * TILE_ROWS, 128), row)),
        scratch_shapes=[pltpu.VMEM((d, d), BF16)],
        compiler_params=_cparams(("arbitrary",)),
        name="oproj_ln",
    )(a, w_all, b_o, x, g_all.reshape(-1, 1, d), b_all.reshape(-1, 1, d))


def _diff_prompt_kernel(lam_ref, sg_ref, q_ref, k_ref, v_ref, o_ref,
                        qt_ref, m_ref, l_ref, acc_ref, *, layer):
    tq = DIFF_TQ
    dh = DIFF_HEAD_DIM
    dv = 2 * dh
    kvh = pl.program_id(1)
    qi = pl.program_id(2)

    qt = (q_ref[...] * (dh ** -0.5)).T
    arow = lax.broadcasted_iota(jnp.int32, (dh, tq), 0)
    q_off = lax.broadcasted_iota(jnp.int32, (dh, tq), 1).astype(F32)
    for g in range(DIFF_GROUP):
        slope = jnp.where(kvh == 0, 2.0 ** -(g + 1), 2.0 ** -(DIFF_GROUP + g + 1)).astype(F32)
        extra = jnp.where(arow < 2, slope, jnp.where(arow == 2, -slope * q_off, 0.0))
        q0 = qt[(g * 2) * dh:(g * 2 + 1) * dh]
        q1 = qt[(g * 2 + 1) * dh:(g * 2 + 2) * dh]
        qt_ref[0, :, g * tq:(g + 1) * tq] = jnp.concatenate([q0, extra], axis=0).astype(BF16)
        qt_ref[1, :, g * tq:(g + 1) * tq] = jnp.concatenate([extra, q1], axis=0).astype(BF16)
    m_ref[...] = jnp.full(m_ref.shape, NEG_INF, F32)
    l_ref[...] = jnp.zeros(l_ref.shape, F32)
    acc_ref[...] = jnp.zeros(acc_ref.shape, F32)

    lane = lax.broadcasted_iota(jnp.int32, (tq, dv), 1)
    k_off = lax.broadcasted_iota(jnp.int32, (tq, dv), 0).astype(F32)
    krow = lax.broadcasted_iota(jnp.int32, (tq, tq), 0)
    qcol = lax.broadcasted_iota(jnp.int32, (tq, tq), 1)
    causal = krow <= qcol

    def block(j, masked):
        start = pl.multiple_of(j * tq, tq)
        kb = k_ref[pl.ds(start, tq), :]
        vt = v_ref[pl.ds(start, tq), :].T.astype(BF16)
        shift = ((j - qi) * tq).astype(F32)
        keys = [
            jnp.where(lane < dh, kb,
                      jnp.where(lane == dh, k_off,
                                jnp.where(lane == dh + 1, shift, jnp.where(lane == dh + 2, 1.0, 0.0)))),
            jnp.where(lane >= dh, kb,
                      jnp.where(lane == 0, k_off,
                                jnp.where(lane == 1, shift, jnp.where(lane == 2, 1.0, 0.0)))),
        ]
        for c in range(2):
            s_all = jnp.dot(keys[c].astype(BF16), qt_ref[c], preferred_element_type=F32)
            for g in range(DIFF_GROUP):
                idx = g * 2 + c
                s = s_all[:, g * tq:(g + 1) * tq]
                if masked:
                    s = jnp.where(causal, s, NEG_INF)
                m_old = m_ref[idx:idx + 1, :]
                m_new = jnp.maximum(m_old, jnp.max(s, axis=0, keepdims=True))
                alpha = jnp.exp(m_old - m_new)
                p = jnp.exp(s - m_new)
                l_ref[idx:idx + 1, :] = alpha * l_ref[idx:idx + 1, :] + jnp.sum(p, axis=0, keepdims=True)
                acc_ref[idx] = alpha * acc_ref[idx] + jnp.dot(vt, p.astype(BF16),
                                                              preferred_element_type=F32)
                m_ref[idx:idx + 1, :] = m_new

    def body(j, carry):
        block(j, False)
        return carry

    lax.fori_loop(0, qi, body, 0)
    block(qi, True)

    lp = lam_ref[...]
    lam = (jnp.exp(jnp.sum(lp[0:1] * lp[1:2], axis=-1, keepdims=True))
           - jnp.exp(jnp.sum(lp[2:3] * lp[3:4], axis=-1, keepdims=True)) + _lambda_init(layer))
    gain = sg_ref[...] * (1.0 - _lambda_init(layer))
    for g in range(DIFF_GROUP):
        o1 = acc_ref[g * 2] * (1.0 / l_ref[g * 2:g * 2 + 1, :])
        o2 = acc_ref[g * 2 + 1] * (1.0 / l_ref[g * 2 + 1:g * 2 + 2, :])
        ot = o1 - lam * o2
        ot = ot * lax.rsqrt(jnp.mean(ot * ot, axis=0, keepdims=True) + LN_EPS)
        o_ref[:, g * dv:(g + 1) * dv] = (ot.T * gain).astype(o_ref.dtype)


def diff_prompt_attention(h, diff_lam, subln_g, slot, layer, n_batch, seq):
    tq = DIFF_TQ
    nq = seq // tq
    qw = DIFF_GROUP * 2 * DIFF_HEAD_DIM
    kw = 2 * DIFF_HEAD_DIM
    return pl.pallas_call(
        functools.partial(_diff_prompt_kernel, layer=layer),
        out_shape=jax.ShapeDtypeStruct((n_batch * seq, DIFF_Q_DIM), BF16),
        grid=(n_batch, DIFF_KV_HEADS, nq),
        in_specs=[
            pl.BlockSpec((None, 4, DIFF_HEAD_DIM), lambda b, h_, i: (slot, 0, 0)),
            pl.BlockSpec((None, 1, kw), lambda b, h_, i: (slot, 0, 0)),
            pl.BlockSpec((tq, qw), lambda b, h_, i: (b * nq + i, h_)),
            pl.BlockSpec((seq, kw), lambda b, h_, i: (b, DIFF_Q_DIM // kw + h_)),
            pl.BlockSpec((seq, kw), lambda b, h_, i: (b, (DIFF_Q_DIM + DIFF_K_DIM) // kw + h_)),
        ],
        out_specs=pl.BlockSpec((tq, qw), lambda b, h_, i: (b * nq + i, h_)),
        scratch_shapes=[
            pltpu.VMEM((2, 2 * DIFF_HEAD_DIM, DIFF_GROUP * tq), BF16),
            pltpu.VMEM((2 * DIFF_GROUP, tq), F32),
            pltpu.VMEM((2 * DIFF_GROUP, tq), F32),
            pltpu.VMEM((2 * DIFF_GROUP, 2 * DIFF_HEAD_DIM, tq), F32),
        ],
        compiler_params=_cparams(("arbitrary", "arbitrary", "arbitrary")),
        name="diff_prompt_attn",
    )(diff_lam, subln_g.reshape(-1, 1, kw), h, h, h)


def _diff_decode_kernel(pt_ref, lam_ref, sg_ref, slope_ref, wq_ref, kn_ref, vn_ref, *rest,
                        layer, n_pages, past_len):
    del pt_ref
    np_ = DECODE_PAGES
    k_refs = rest[:np_]
    v_refs = rest[np_:2 * np_]
    o_ref = rest[2 * np_]
    m_ref, l_ref, acc_ref = rest[2 * np_ + 1:]
    jc = pl.program_id(1)
    n_chunks = n_pages // np_
    dv = 2 * DIFF_HEAD_DIM

    @pl.when(jc == 0)
    def _():
        m_ref[...] = jnp.full(m_ref.shape, NEG_INF, F32)
        l_ref[...] = jnp.zeros(l_ref.shape, F32)
        acc_ref[...] = jnp.zeros(acc_ref.shape, F32)

    wq = wq_ref[...]
    slope = slope_ref[...]
    lane = lax.broadcasted_iota(jnp.int32, (1, PAGE_SIZE), 1)
    rowi = lax.broadcasted_iota(jnp.int32, (16, 1), 0)
    is_kv0 = (rowi % (2 * DIFF_GROUP)) < DIFF_GROUP

    s_pages = []
    for g in range(np_):
        kt = k_refs[g][...].astype(BF16)
        s = jnp.dot(wq, kt, preferred_element_type=F32)
        k_pos = (jc * np_ + g) * PAGE_SIZE + lane
        s_pages.append(s - slope * (past_len - k_pos).astype(F32))
    m_old = m_ref[...]
    m_new = m_old
    for s in s_pages:
        m_new = jnp.maximum(m_new, jnp.max(s, axis=-1, keepdims=True))
    alpha = jnp.exp(m_old - m_new)
    l_new = alpha * l_ref[...]
    acc = alpha * acc_ref[...]
    for g in range(np_):
        p = jnp.exp(s_pages[g] - m_new)
        l_new = l_new + jnp.sum(p, axis=-1, keepdims=True)
        pb = p.astype(BF16)
        v0 = v_refs[g][pl.ds(0, PAGE_SIZE, stride=DIFF_KV_HEADS), :].astype(BF16)
        v1 = v_refs[g][pl.ds(1, PAGE_SIZE, stride=DIFF_KV_HEADS), :].astype(BF16)
        acc = acc + jnp.where(is_kv0, jnp.dot(pb, v0, preferred_element_type=F32),
                              jnp.dot(pb, v1, preferred_element_type=F32))
    m_ref[...] = m_new
    l_ref[...] = l_new
    acc_ref[...] = acc

    @pl.when(jc == n_chunks - 1)
    def _():
        kn = kn_ref[...]
        vn = vn_ref[...]
        s_n = jnp.sum(wq.astype(F32) * kn, axis=-1, keepdims=True)
        m_o = m_ref[...]
        m_f = jnp.maximum(m_o, s_n)
        a_f = jnp.exp(m_o - m_f)
        p_n = jnp.exp(s_n - m_f)
        l_f = a_f * l_ref[...] + p_n
        v_rows = jnp.where(is_kv0, vn[:, :dv], vn[:, dv:])
        acc_f = a_f * acc_ref[...] + p_n * v_rows
        o_maps = acc_f / l_f
        lp = lam_ref[...]
        lam = (jnp.exp(jnp.sum(lp[0:1] * lp[1:2], axis=-1, keepdims=True))
               - jnp.exp(jnp.sum(lp[2:3] * lp[3:4], axis=-1, keepdims=True)) + _lambda_init(layer))
        o = o_maps[:DIFF_HEADS] - lam * o_maps[DIFF_HEADS:]
        gain = sg_ref[...] * (1.0 - _lambda_init(layer))
        o = o * lax.rsqrt(jnp.mean(o * o, axis=-1, keepdims=True) + LN_EPS) * gain
        o_ref[...] = o.astype(o_ref.dtype)


def diff_decode_attention(hs, kt_cache, v_cache, page_table, diff_lam, subln_g, slot, layer):
    n_seq = hs.shape[0]
    n_pages = page_table.shape[1]
    past_len = n_pages * PAGE_SIZE
    dh = DIFF_HEAD_DIM
    q = hs[:, :DIFF_Q_DIM].reshape(n_seq, DIFF_KV_HEADS, DIFF_GROUP, 2, dh) * (dh ** -0.5)
    eye = jnp.eye(2, dtype=F32)
    wq = jnp.einsum('bkgcd,kK,cC->bckgKCd', q, eye, eye).reshape(n_seq, 16, DIFF_K_DIM).astype(BF16)
    kn = hs[:, DIFF_Q_DIM:DIFF_Q_DIM + DIFF_K_DIM].reshape(n_seq, 1, DIFF_K_DIM)
    vn = hs[:, DIFF_Q_DIM + DIFF_K_DIM:].reshape(n_seq, 1, DIFF_V_DIM)
    heads = np.arange(DIFF_HEADS, dtype=np.float32).reshape(DIFF_KV_HEADS, DIFF_GROUP)
    slopes = np.tile((2.0 ** -(heads + 1.0)).reshape(1, DIFF_HEADS), (2, 1)).reshape(16, 1)
    slopes = jnp.asarray(slopes, F32)

    np_ = DECODE_PAGES
    n_chunks = n_pages // np_

    def page_map(g):
        return lambda b, jc, pt: (slot, pt[b, jc * np_ + g], 0, 0)

    in_specs = [
        pl.BlockSpec((None, 4, dh), lambda b, jc, pt: (slot, 0, 0)),
        pl.BlockSpec((None, 1, 2 * dh), lambda b, jc, pt: (slot, 0, 0)),
        pl.BlockSpec((16, 1), lambda b, jc, pt: (0, 0)),
        pl.BlockSpec((None, 16, DIFF_K_DIM), lambda b, jc, pt: (b, 0, 0)),
        pl.BlockSpec((None, 1, DIFF_K_DIM), lambda b, jc, pt: (b, 0, 0)),
        pl.BlockSpec((None, 1, DIFF_V_DIM), lambda b, jc, pt: (b, 0, 0)),
    ]
    in_specs += [pl.BlockSpec((None, None, DIFF_K_DIM, PAGE_SIZE), page_map(g)) for g in range(np_)]
    in_specs += [pl.BlockSpec((None, None, PAGE_SIZE * DIFF_KV_HEADS, 2 * dh), page_map(g))
                 for g in range(np_)]
    out = pl.pallas_call(
        functools.partial(_diff_decode_kernel, layer=layer, n_pages=n_pages, past_len=past_len),
        out_shape=jax.ShapeDtypeStruct((n_seq, DIFF_HEADS, 2 * dh), BF16),
        grid_spec=pltpu.PrefetchScalarGridSpec(
            num_scalar_prefetch=1,
            grid=(n_seq, n_chunks),
            in_specs=in_specs,
            out_specs=pl.BlockSpec((None, DIFF_HEADS, 2 * dh), lambda b, jc, pt: (b, 0, 0)),
            scratch_shapes=[
                pltpu.VMEM((16, 1), F32),
                pltpu.VMEM((16, 1), F32),
                pltpu.VMEM((16, 2 * dh), F32),
            ],
        ),
        compiler_params=_cparams(("arbitrary", "arbitrary")),
        name="diff_decode_attn",
    )(page_table, diff_lam, subln_g.reshape(-1, 1, 2 * dh), slopes, wq, kn, vn,
      *([kt_cache] * np_), *([v_cache] * np_))
    return out.reshape(n_seq, DIFF_Q_DIM)


def _swa_slopes():
    return 2.0 ** (-8.0 * np.arange(1, SWA_HEADS + 1, dtype=np.float32) / SWA_HEADS)


def _swa_prompt_kernel(sink_ref, slope_ref, q_ref, cur_ref, prev_ref, o_ref, *, n_blocks):
    w = WINDOW
    dh = SWA_HEAD_DIM
    bi = pl.program_id(0) % n_blocks
    krow = lax.broadcasted_iota(jnp.int32, (2 * w, w), 0)
    qcol = lax.broadcasted_iota(jnp.int32, (2 * w, w), 1)
    dist = (qcol - krow + w).astype(F32)
    in_seq = jnp.logical_or(krow >= w, (jnp.zeros_like(krow) + bi) > 0)
    ok = jnp.logical_and(jnp.logical_and(krow > qcol, krow <= qcol + w), in_seq)

    qt = (q_ref[...] * (dh ** -0.5)).T.astype(BF16)
    kk = jnp.concatenate([prev_ref[:, :SWA_KV_DIM], cur_ref[:, :SWA_KV_DIM]], axis=0).astype(BF16)
    vt = jnp.concatenate([prev_ref[:, SWA_KV_DIM:], cur_ref[:, SWA_KV_DIM:]], axis=0).T.astype(BF16)
    for kh in range(SWA_KV_HEADS):
        q_heads = jnp.concatenate(
            [qt[(kh * SWA_GROUP + g) * dh:(kh * SWA_GROUP + g + 1) * dh] for g in range(SWA_GROUP)],
            axis=1)
        s_all = jnp.dot(kk[:, kh * dh:(kh + 1) * dh], q_heads, preferred_element_type=F32)
        probs = []
        for g in range(SWA_GROUP):
            hd = kh * SWA_GROUP + g
            sink = sink_ref[hd]
            s = jnp.where(ok, s_all[:, g * w:(g + 1) * w] - slope_ref[hd] * dist, NEG_INF)
            m = jnp.maximum(jnp.max(s, axis=0, keepdims=True), sink)
            e = jnp.exp(s - m)
            den = jnp.sum(e, axis=0, keepdims=True) + jnp.exp(sink - m)
            probs.append((e * (1.0 / den)).astype(BF16))
        ot = jnp.dot(vt[kh * dh:(kh + 1) * dh], jnp.concatenate(probs, axis=1),
                     preferred_element_type=F32)
        for g in range(0, SWA_GROUP, 2):
            pair = jnp.concatenate([ot[:, g * w:(g + 1) * w], ot[:, (g + 1) * w:(g + 2) * w]], axis=0)
            hd = kh * SWA_GROUP + g
            o_ref[:, hd * dh:(hd + 2) * dh] = pair.T.astype(o_ref.dtype)


def swa_prompt_attention(h, sinks, n_batch, seq):
    nb = seq // WINDOW
    kvw = 2 * SWA_KV_DIM
    kv_col = SWA_Q_DIM // kvw
    slopes = jnp.asarray(_swa_slopes(), F32)
    smem = pl.BlockSpec(memory_space=pltpu.SMEM)
    return pl.pallas_call(
        functools.partial(_swa_prompt_kernel, n_blocks=nb),
        out_shape=jax.ShapeDtypeStruct((n_batch * seq, SWA_Q_DIM), BF16),
        grid=(n_batch * nb,),
        in_specs=[
            smem, smem,
            pl.BlockSpec((WINDOW, SWA_Q_DIM), lambda r: (r, 0)),
            pl.BlockSpec((WINDOW, kvw), lambda r: (r, kv_col)),
            pl.BlockSpec((WINDOW, kvw), lambda r: (jnp.maximum(r - 1, 0), kv_col)),
        ],
        out_specs=pl.BlockSpec((WINDOW, SWA_Q_DIM), lambda r: (r, 0)),
        compiler_params=_cparams(("arbitrary",)),
        name="swa_prompt_attn",
    )(sinks, slopes, h, h, h)


def _swa_decode_kernel(sink_ref, slope_ref, wq_ref, kn_ref, vn_ref, kt_ref, vt_ref, o_ref, *, win_buf):
    dh = SWA_HEAD_DIM
    lane = lax.broadcasted_iota(jnp.int32, (1, win_buf), 1)
    dist = (win_buf - lane).astype(F32)
    ok = dist < float(WINDOW)
    rowi = lax.broadcasted_iota(jnp.int32, (SWA_HEADS, 1), 0)
    is_kv0 = rowi < SWA_GROUP
    sink = sink_ref[...]
    slope = slope_ref[...]
    for sq in range(SWA_SEQS):
        wq = wq_ref[sq]
        kt = kt_ref[sq].astype(BF16)
        vt = vt_ref[sq].astype(BF16)
        s = jnp.dot(wq, kt, preferred_element_type=F32)
        s = jnp.where(ok, s - slope * dist, NEG_INF)
        s_n = jnp.sum(wq.astype(F32) * kn_ref[sq], axis=-1, keepdims=True)
        m = jnp.maximum(jnp.maximum(jnp.max(s, axis=-1, keepdims=True), s_n), sink)
        e = jnp.exp(s - m)
        e_n = jnp.exp(s_n - m)
        den = jnp.sum(e, axis=-1, keepdims=True) + e_n + jnp.exp(sink - m)
        inv = 1.0 / den
        pv = lax.dot_general((e * inv).astype(BF16), vt, (((1,), (1,)), ((), ())),
                             preferred_element_type=F32)
        pv = pv + (e_n * inv) * vn_ref[sq]
        o_ref[sq] = jnp.where(is_kv0, pv[:, :dh], pv[:, dh:]).astype(o_ref.dtype)


def swa_decode_attention(hs, kt_buf, vt_buf, sinks):
    n_seq = hs.shape[0]
    win_buf = kt_buf.shape[-1]
    dh = SWA_HEAD_DIM
    q = hs[:, :SWA_Q_DIM].reshape(n_seq, SWA_KV_HEADS, SWA_GROUP, dh) * (dh ** -0.5)
    eye = jnp.eye(SWA_KV_HEADS, dtype=F32)
    wq = jnp.einsum('bkgd,kK->bkgKd', q, eye).reshape(n_seq, SWA_HEADS, SWA_KV_DIM).astype(BF16)
    kn = hs[:, SWA_Q_DIM:SWA_Q_DIM + SWA_KV_DIM].reshape(n_seq, 1, SWA_KV_DIM)
    vn = hs[:, SWA_Q_DIM + SWA_KV_DIM:].reshape(n_seq, 1, SWA_KV_DIM)
    slopes = jnp.asarray(_swa_slopes().reshape(SWA_HEADS, 1), F32)
    sb = SWA_SEQS
    seq3 = lambda i: (i, 0, 0)
    out = pl.pallas_call(
        functools.partial(_swa_decode_kernel, win_buf=win_buf),
        out_shape=jax.ShapeDtypeStruct((n_seq, SWA_HEADS, dh), BF16),
        grid=(n_seq // sb,),
        in_specs=[
            pl.BlockSpec((SWA_HEADS, 1), lambda i: (0, 0)),
            pl.BlockSpec((SWA_HEADS, 1), lambda i: (0, 0)),
            pl.BlockSpec((sb, SWA_HEADS, SWA_KV_DIM), seq3),
            pl.BlockSpec((sb, 1, SWA_KV_DIM), seq3),
            pl.BlockSpec((sb, 1, SWA_KV_DIM), seq3),
            pl.BlockSpec((sb, SWA_KV_DIM, win_buf), seq3),
            pl.BlockSpec((sb, SWA_KV_DIM, win_buf), seq3),
        ],
        out_specs=pl.BlockSpec((sb, SWA_HEADS, dh), seq3),
        compiler_params=_cparams(("arbitrary",)),
        name="swa_decode_attn",
    )(sinks.reshape(SWA_HEADS, 1), slopes, wq, kn, vn, kt_buf, vt_buf)
    return out.reshape(n_seq, SWA_Q_DIM)


def _router_kernel(x_ref, w_ref, b_ref, e_ref, g_ref, r_ref, cnt_ref, carry_ref):
    tm = x_ref.shape[0]

    @pl.when(pl.program_id(0) == 0)
    def _():
        carry_ref[...] = jnp.zeros(carry_ref.shape, F32)

    logits = jnp.dot(x_ref[...].astype(BF16), w_ref[...].astype(BF16),
                     preferred_element_type=F32) + b_ref[...]
    lane = lax.broadcasted_iota(jnp.int32, (tm, N_EXPERTS), 1).astype(F32)
    work = logits
    vals, idxs = [], []
    for _ in range(TOP_K):
        mx = jnp.max(work, axis=-1, keepdims=True)
        ix = jnp.min(jnp.where(work == mx, lane, float(N_EXPERTS)), axis=-1, keepdims=True)
        vals.append(mx)
        idxs.append(ix)
        work = jnp.where(lane == ix, -jnp.inf, work)
    ex = [jnp.exp(v - vals[0]) for v in vals]
    den = ex[0] + ex[1] + ex[2] + ex[3]

    onehot = jnp.zeros((tm, N_EXPERTS), F32)
    for ix in idxs:
        onehot = onehot + (lane == ix).astype(F32)
    ri = lax.broadcasted_iota(jnp.int32, (tm, tm), 0)
    ci = lax.broadcasted_iota(jnp.int32, (tm, tm), 1)
    tri = (ci < ri).astype(BF16)
    before = jnp.dot(tri, onehot.astype(BF16), preferred_element_type=F32) + carry_ref[...]

    col4 = lax.broadcasted_iota(jnp.int32, (tm, TOP_K), 1)
    e_out = jnp.zeros((tm, TOP_K), jnp.int32)
    g_out = jnp.zeros((tm, TOP_K), F32)
    r_out = jnp.zeros((tm, TOP_K), jnp.int32)
    for k in range(TOP_K):
        rank = jnp.sum(jnp.where(lane == idxs[k], before, 0.0), axis=-1, keepdims=True)
        e_out = jnp.where(col4 == k, idxs[k].astype(jnp.int32), e_out)
        g_out = jnp.where(col4 == k, ex[k] / den, g_out)
        r_out = jnp.where(col4 == k, rank.astype(jnp.int32), r_out)
    e_ref[...] = e_out
    g_ref[...] = g_out
    r_ref[...] = r_out
    carry_ref[...] = carry_ref[...] + jnp.sum(onehot, axis=0, keepdims=True)
    cnt_ref[...] = carry_ref[...]


def router(x, w_router, b_router, layer):
    t, d = x.shape
    row = lambda i: (i, 0)
    return pl.pallas_call(
        _router_kernel,
        out_shape=(
            jax.ShapeDtypeStruct((t, TOP_K), jnp.int32),
            jax.ShapeDtypeStruct((t, TOP_K), F32),
            jax.ShapeDtypeStruct((t, TOP_K), jnp.int32),
            jax.ShapeDtypeStruct((1, N_EXPERTS), F32),
        ),
        grid=(t // ROW_TILE,),
        in_specs=[
            pl.BlockSpec((ROW_TILE, d), row),
            pl.BlockSpec((None, d, N_EXPERTS), lambda i: (layer, 0, 0)),
            pl.BlockSpec((None, 1, N_EXPERTS), lambda i: (layer, 0, 0)),
        ],
        out_specs=(
            pl.BlockSpec((ROW_TILE, TOP_K), row),
            pl.BlockSpec((ROW_TILE, TOP_K), row),
            pl.BlockSpec((ROW_TILE, TOP_K), row),
            pl.BlockSpec((1, N_EXPERTS), lambda i: (0, 0)),
        ),
        scratch_shapes=[pltpu.VMEM((1, N_EXPERTS), F32)],
        compiler_params=_cparams(("arbitrary",)),
        name="router",
    )(x, w_router, b_router.reshape(-1, 1, N_EXPERTS))


MAX_COPIES_PER_STEP = 2048


def _copies_per_step(n):
    return max(c for c in range(8, MAX_COPIES_PER_STEP + 1, 8) if n % c == 0)


def _token_copy(src_ref, dst_ref, sem, src_tok, dst_tok):
    return pltpu.make_async_copy(src_ref.at[src_tok], dst_ref.at[dst_tok], sem)


def _drain_token_copies(src_ref, dst_ref, sem, n):
    def drain(i, carry):
        _token_copy(src_ref, dst_ref, sem, 0, 0).wait()
        return carry

    lax.fori_loop(0, n, drain, 0, unroll=8)


def _scatter_tokens_kernel(idx_ref, src_ref, init_ref, out_ref, sem, *, n):
    del init_ref
    base = pl.program_id(0) * n

    def issue(i, carry):
        tok = lax.shift_right_logical(base + i, TOP_K.bit_length() - 1)
        _token_copy(src_ref, out_ref, sem, tok, idx_ref[0, i]).start()
        return carry

    lax.fori_loop(0, n, issue, 0, unroll=8)
    _drain_token_copies(src_ref, out_ref, sem, n)


def scatter_tokens(src_tiles, idx, n_out):
    t = src_tiles.shape[0] // TILE_ROWS
    n = _copies_per_step(idx.shape[0])
    steps = idx.shape[0] // n
    any_spec = pl.BlockSpec(memory_space=pl.ANY)
    out = pl.pallas_call(
        functools.partial(_scatter_tokens_kernel, n=n),
        out_shape=jax.ShapeDtypeStruct((n_out, TILE_ROWS, 128), F32),
        grid=(steps,),
        in_specs=[pl.BlockSpec((None, 1, n), lambda s: (s, 0, 0), memory_space=pltpu.SMEM),
                  any_spec, any_spec],
        out_specs=any_spec,
        scratch_shapes=[pltpu.SemaphoreType.DMA(())],
        input_output_aliases={2: 0},
        compiler_params=_cparams(("arbitrary",)),
        name="scatter_tokens",
    )(idx.reshape(steps, 1, n), src_tiles.reshape(t, TILE_ROWS, 128),
      jnp.zeros((n_out, TILE_ROWS, 128), F32))
    return out.reshape(n_out * TILE_ROWS, 128)


def _gather_tokens_kernel(idx_ref, src_ref, out_ref, sem, *, n):
    base = pl.program_id(0) * n

    def issue(i, carry):
        _token_copy(src_ref, out_ref, sem, idx_ref[0, i], base + i).start()
        return carry

    lax.fori_loop(0, n, issue, 0, unroll=8)
    _drain_token_copies(src_ref, out_ref, sem, n)


def gather_tokens(src_tiles, idx):
    r = src_tiles.shape[0] // TILE_ROWS
    n_idx = idx.shape[0]
    n = _copies_per_step(n_idx)
    steps = n_idx // n
    any_spec = pl.BlockSpec(memory_space=pl.ANY)
    out = pl.pallas_call(
        functools.partial(_gather_tokens_kernel, n=n),
        out_shape=jax.ShapeDtypeStruct((n_idx, TILE_ROWS, 128), F32),
        grid=(steps,),
        in_specs=[pl.BlockSpec((None, 1, n), lambda s: (s, 0, 0), memory_space=pltpu.SMEM),
                  any_spec],
        out_specs=any_spec,
        scratch_shapes=[pltpu.SemaphoreType.DMA(())],
        compiler_params=_cparams(("arbitrary",)),
        name="gather_tokens",
    )(idx.reshape(steps, 1, n), src_tiles.reshape(r, TILE_ROWS, 128))
    return out.reshape(n_idx * TILE_ROWS, 128)


def _experts_kernel(be_ref, nu_ref, x_ref, wgu_ref, bgu_ref, wd_ref, bd_ref, o_ref, wgu_bf, wd_bf):
    s = pl.program_id(0)
    prev = be_ref[jnp.maximum(s - 1, 0)]
    new_expert = jnp.logical_or(s == 0, be_ref[s] != prev)
    used = s < nu_ref[0]

    @pl.when(jnp.logical_and(new_expert, used))
    def _():
        wgu_bf[...] = wgu_ref[...].astype(BF16)
        wd_bf[...] = wd_ref[...].astype(BF16)

    @pl.when(used)
    def _():
        x = _load_token_tiles(x_ref).astype(BF16)
        h = jnp.dot(x, wgu_bf[...], preferred_element_type=F32) + bgu_ref[...]
        gate = jnp.minimum(h[:, :D_FF], SWIGLU_LIMIT)
        up = jnp.clip(h[:, D_FF:], -SWIGLU_LIMIT, SWIGLU_LIMIT)
        act = (up + 1.0) * (gate * jax.nn.sigmoid(SWIGLU_ALPHA * gate))
        y = jnp.dot(act.astype(BF16), wd_bf[...], preferred_element_type=F32) + bd_ref[...]
        _store_token_tiles(o_ref, y)

    @pl.when(jnp.logical_not(used))
    def _():
        o_ref[...] = jnp.zeros(o_ref.shape, o_ref.dtype)


def experts(xs, block_e, n_used, w_gu, b_gu, w_down, b_down, layer):
    d = D_MODEL
    n_blocks = xs.shape[0] // (MOE_TM * TILE_ROWS)
    tile_block = pl.BlockSpec((MOE_TM * TILE_ROWS, 128), lambda s, be, nu: (s, 0))
    return pl.pallas_call(
        _experts_kernel,
        out_shape=jax.ShapeDtypeStruct(xs.shape, F32),
        grid_spec=pltpu.PrefetchScalarGridSpec(
            num_scalar_prefetch=2,
            grid=(n_blocks,),
            in_specs=[
                tile_block,
                pl.BlockSpec((None, None, d, 2 * D_FF), lambda s, be, nu: (layer, be[s], 0, 0)),
                pl.BlockSpec((None, None, 1, 2 * D_FF), lambda s, be, nu: (layer, be[s], 0, 0)),
                pl.BlockSpec((None, None, D_FF, d), lambda s, be, nu: (layer, be[s], 0, 0)),
                pl.BlockSpec((None, None, 1, d), lambda s, be, nu: (layer, be[s], 0, 0)),
            ],
            out_specs=tile_block,
            scratch_shapes=[
                pltpu.VMEM((d, 2 * D_FF), BF16),
                pltpu.VMEM((D_FF, d), BF16),
            ],
        ),
        compiler_params=_cparams(("arbitrary",)),
        name="experts",
    )(block_e, n_used, xs, w_gu, b_gu.reshape(DEPTH, N_EXPERTS, 1, 2 * D_FF),
      w_down, b_down.reshape(DEPTH, N_EXPERTS, 1, d))


def _combine_ln_kernel(x_ref, y0_ref, y1_ref, y2_ref, y3_ref, gt_ref, g_ref, b_ref, o_ref):
    gt = gt_ref[...]
    f = _load_token_tiles(y0_ref) * gt[:, 0:1]
    for k, y_ref in enumerate((y1_ref, y2_ref, y3_ref), start=1):
        f = f + _load_token_tiles(y_ref) * gt[:, k:k + 1]
    o_ref[...] = _layer_norm_rows(ALPHA * x_ref[...] + f, g_ref[...], b_ref[...])


def combine_ln(x, y4, gates, g_all, b_all, layer):
    t, d = x.shape
    nt = t // ROW_TILE
    row = lambda i: (i, 0)

    def y_spec(k):
        return pl.BlockSpec((ROW_TILE * TILE_ROWS, 128), lambda i: (k * nt + i, 0))

    return pl.pallas_call(
        _combine_ln_kernel,
        out_shape=jax.ShapeDtypeStruct((t, d), F32),
        grid=(nt,),
        in_specs=[
            pl.BlockSpec((ROW_TILE, d), row),
            y_spec(0), y_spec(1), y_spec(2), y_spec(3),
            pl.BlockSpec((ROW_TILE, TOP_K), row),
            pl.BlockSpec((None, 1, d), lambda i: (layer, 0, 0)),
            pl.BlockSpec((None, 1, d), lambda i: (layer, 0, 0)),
        ],
        out_specs=pl.BlockSpec((ROW_TILE, d), row),
        compiler_params=_cparams(("arbitrary",)),
        name="combine_ln",
    )(x, y4, y4, y4, y4, gates, g_all.reshape(-1, 1, d), b_all.reshape(-1, 1, d))


def moe_layer(x, x_tiles, layer, w_router, b_router, w_gu, b_gu, w_down, b_down, ln_g, ln_b):
    t, d = x.shape
    top_e, gates, rank, counts = router(x, w_router, b_router, layer)
    counts = counts.reshape(N_EXPERTS).astype(jnp.int32)
    padded = (counts + MOE_TM - 1) // MOE_TM * MOE_TM
    pad_end = jnp.cumsum(padded)
    pad_start = pad_end - padded
    n_blocks = -(-t * TOP_K // MOE_TM) + N_EXPERTS
    dest = pad_start[top_e] + rank
    block_start = jnp.arange(n_blocks, dtype=jnp.int32) * MOE_TM
    block_e = jnp.minimum(jnp.sum(pad_end[None, :] <= block_start[:, None], axis=1),
                          N_EXPERTS - 1).astype(jnp.int32)
    n_used = (pad_end[-1] // MOE_TM).astype(jnp.int32).reshape(1)
    xs = scatter_tokens(x_tiles, dest.reshape(-1), n_blocks * MOE_TM)
    ys = experts(xs, block_e, n_used, w_gu, b_gu, w_down, b_down, layer)
    y4 = gather_tokens(ys, dest.T.reshape(-1))
    return combine_ln(x, y4, gates, ln_g, ln_b, layer)


def kernel(x_prompt, x_sample, cache_k_diff, cache_v_diff, cache_k_swa, cache_v_swa, page_table,
           ln1_g, ln1_b, ln2_g, ln2_b, w_qkv_diff, diff_lam, diff_subln_g, w_o_diff,
           w_qkv_swa, b_qkv_swa, swa_sinks, w_o_swa, b_o_swa,
           w_router, b_router, w_gu, b_gu, w_down, b_down):
    n_batch, seq, d = x_prompt.shape
    n_seq = x_sample.shape[0]
    tp = n_batch * seq
    x = jnp.concatenate([x_prompt.reshape(tp, d), x_sample.reshape(n_seq, d)], axis=0)

    n_layers_diff, n_phys = cache_k_diff.shape[:2]
    kt_cache = jnp.transpose(cache_k_diff, (0, 1, 3, 4, 5, 2)).reshape(
        n_layers_diff, n_phys, DIFF_K_DIM, PAGE_SIZE)
    v_cache = cache_v_diff.reshape(n_layers_diff, n_phys, PAGE_SIZE * DIFF_KV_HEADS, 2 * DIFF_HEAD_DIM)
    win_buf = cache_k_swa.shape[2]
    kt_swa = jnp.transpose(cache_k_swa, (0, 1, 3, 4, 2)).reshape(-1, n_seq, SWA_KV_DIM, win_buf)
    vt_swa = jnp.transpose(cache_v_swa, (0, 1, 3, 4, 2)).reshape(-1, n_seq, SWA_KV_DIM, win_buf)

    zero_bias_qkv = jnp.zeros((1, w_qkv_diff.shape[-1]), F32)
    zero_bias_o = jnp.zeros((1, d), F32)

    kdp, vdp, kds, vds = [], [], [], []
    ksp, vsp, kss, vss = [], [], [], []
    for i in range(DEPTH):
        j = i // 2
        if i % 2 == 0:
            h = linear(x, w_qkv_diff, j, zero_bias_qkv, 512)
            hs = h[tp:]
            a_p = diff_prompt_attention(h, diff_lam, diff_subln_g, j, i, n_batch, seq)
            a_s = diff_decode_attention(hs, kt_cache, v_cache, page_table, diff_lam, diff_subln_g, j, i)
            kq, vq = DIFF_Q_DIM, DIFF_Q_DIM + DIFF_K_DIM
            kdp.append(h[:tp, kq:vq].reshape(n_batch, seq, DIFF_KV_HEADS, 2, DIFF_HEAD_DIM))
            vdp.append(h[:tp, vq:].reshape(n_batch, seq, DIFF_KV_HEADS, 2 * DIFF_HEAD_DIM))
            kds.append(hs[:, kq:vq].reshape(n_seq, 1, DIFF_KV_HEADS, 2, DIFF_HEAD_DIM))
            vds.append(hs[:, vq:].reshape(n_seq, 1, DIFF_KV_HEADS, 2 * DIFF_HEAD_DIM))
            a = jnp.concatenate([a_p, a_s], axis=0)
            x, x_tiles = oproj_ln(a, w_o_diff, j, zero_bias_o, x, ln1_g, ln1_b, i)
        else:
            h = linear(x, w_qkv_swa, j, b_qkv_swa[j].reshape(1, -1), 640)
            hs = h[tp:]
            a_p = swa_prompt_attention(h, swa_sinks[j], n_batch, seq)
            a_s = swa_decode_attention(hs, kt_swa[j], vt_swa[j], swa_sinks[j])
            kq, vq = SWA_Q_DIM, SWA_Q_DIM + SWA_KV_DIM
            keep = min(WINDOW, seq)
            hp = h[:tp].reshape(n_batch, seq, -1)[:, seq - keep:]
            ksp.append(hp[..., kq:vq].reshape(n_batch, keep, SWA_KV_HEADS, SWA_HEAD_DIM))
            vsp.append(hp[..., vq:].reshape(n_batch, keep, SWA_KV_HEADS, SWA_HEAD_DIM))
            k_new = hs[:, kq:vq].reshape(n_seq, 1, SWA_KV_HEADS, SWA_HEAD_DIM)
            v_new = hs[:, vq:].reshape(n_seq, 1, SWA_KV_HEADS, SWA_HEAD_DIM)
            kss.append(jnp.concatenate([cache_k_swa[j], k_new], axis=1)[:, -win_buf:])
            vss.append(jnp.concatenate([cache_v_swa[j], v_new], axis=1)[:, -win_buf:])
            a = jnp.concatenate([a_p, a_s], axis=0)
            x, x_tiles = oproj_ln(a, w_o_swa, j, b_o_swa[j].reshape(1, -1), x, ln1_g, ln1_b, i)
        x = moe_layer(x, x_tiles, i, w_router, b_router, w_gu, b_gu, w_down, b_down, ln2_g, ln2_b)

    return (x[:tp].reshape(n_batch, seq, d), x[tp:].reshape(n_seq, 1, d),
            jnp.stack(kdp), jnp.stack(vdp), jnp.stack(kds), jnp.stack(vds),
            jnp.stack(ksp), jnp.stack(vsp), jnp.stack(kss), jnp.stack(vss))
```

```python
import functools
import math

import jax
import jax.numpy as jnp
import numpy as np
from jax import lax
from jax.experimental import pallas as pl
from jax.experimental.pallas import tpu as pltpu

F32 = jnp.float32
BF16 = jnp.bfloat16

D_MODEL = 1024
DEPTH = 4
PAGE_SIZE = 128
DIFF_HEAD_DIM = 64
DIFF_HEADS = 8
DIFF_KV_HEADS = 2
DIFF_GROUP = 4
DIFF_Q_DIM = 1024
DIFF_K_DIM = 256
DIFF_V_DIM = 256
SWA_HEAD_DIM = 64
SWA_HEADS = 16
SWA_KV_HEADS = 2
SWA_GROUP = 8
SWA_Q_DIM = 1024
SWA_KV_DIM = 128
WINDOW = 128
N_EXPERTS = 32
TOP_K = 4
D_FF = 1024
SWIGLU_ALPHA = 1.702
SWIGLU_LIMIT = 7.0
ALPHA = (2 * DEPTH) ** 0.25
LN_EPS = 1e-5
NEG_INF = -1e30

ROW_TILE = 384
DIFF_TQ = 256
MOE_TM = 512
DECODE_PAGES = 16
SWA_SEQS = 8
VMEM_LIMIT = 56 * 1024 * 1024


def _lambda_init(layer):
    return 0.8 - 0.6 * math.exp(-0.3 * layer)


def _cparams(sem):
    return pltpu.CompilerParams(dimension_semantics=sem, vmem_limit_bytes=VMEM_LIMIT)


def _linear_kernel(x_ref, w_ref, b_ref, o_ref, wbf_ref):
    @pl.when(pl.program_id(1) == 0)
    def _():
        wbf_ref[...] = w_ref[...].astype(BF16)

    acc = jnp.dot(x_ref[...].astype(BF16), wbf_ref[...], preferred_element_type=F32)
    o_ref[...] = acc + b_ref[...]


def linear(x, w_all, layer, bias, tn):
    t, k = x.shape
    n = w_all.shape[-1]
    return pl.pallas_call(
        _linear_kernel,
        out_shape=jax.ShapeDtypeStruct((t, n), F32),
        grid=(n // tn, t // ROW_TILE),
        in_specs=[
            pl.BlockSpec((ROW_TILE, k), lambda j, i: (i, 0)),
            pl.BlockSpec((None, k, tn), lambda j, i: (layer, 0, j)),
            pl.BlockSpec((1, tn), lambda j, i: (0, j)),
        ],
        out_specs=pl.BlockSpec((ROW_TILE, tn), lambda j, i: (i, j)),
        scratch_shapes=[pltpu.VMEM((k, tn), BF16)],
        compiler_params=_cparams(("arbitrary", "arbitrary")),
        name="linear",
    )(x, w_all, bias)


def _layer_norm_rows(z, g, b):
    mu = jnp.mean(z, axis=-1, keepdims=True)
    zc = z - mu
    var = jnp.mean(zc * zc, axis=-1, keepdims=True)
    return zc * lax.rsqrt(var + LN_EPS) * g + b


TILE_ROWS = D_MODEL // 128


def _store_token_tiles(ref, val):
    n = val.shape[0]
    for c in range(TILE_ROWS):
        ref[pl.ds(c, n, stride=TILE_ROWS), :] = val[:, c * 128:(c + 1) * 128]


def _load_token_tiles(ref):
    n = ref.shape[0] // TILE_ROWS
    return jnp.concatenate([ref[pl.ds(c, n, stride=TILE_ROWS), :] for c in range(TILE_ROWS)], axis=-1)


def _oproj_ln_kernel(a_ref, w_ref, bo_ref, x_ref, g_ref, b_ref, o_ref, ot_ref, wbf_ref):
    @pl.when(pl.program_id(0) == 0)
    def _():
        wbf_ref[...] = w_ref[...].astype(BF16)

    y = jnp.dot(a_ref[...], wbf_ref[...], preferred_element_type=F32) + bo_ref[...]
    z = _layer_norm_rows(ALPHA * x_ref[...] + y, g_ref[...], b_ref[...])
    o_ref[...] = z
    _store_token_tiles(ot_ref, z)


def oproj_ln(a, w_all, layer_slot, b_o, x, g_all, b_all, layer):
    t, d = x.shape
    row = lambda i: (i, 0)
    return pl.pallas_call(
        _oproj_ln_kernel,
        out_shape=(jax.ShapeDtypeStruct((t, d), F32),
                   jax.ShapeDtypeStruct((t * TILE_ROWS, 128), F32)),
        grid=(t // ROW_TILE,),
        in_specs=[
            pl.BlockSpec((ROW_TILE, d), row),
            pl.BlockSpec((None, d, d), lambda i: (layer_slot, 0, 0)),
            pl.BlockSpec((1, d), lambda i: (0, 0)),
            pl.BlockSpec((ROW_TILE, d), row),
            pl.BlockSpec((None, 1, d), lambda i: (layer, 0, 0)),
            pl.BlockSpec((None, 1, d), lambda i: (layer, 0, 0)),
        ],
        out_specs=(pl.BlockSpec((ROW_TILE, d), row),
                   pl.BlockSpec((ROW_TILE * TILE_ROWS, 128), row)),
        scratch_shapes=[pltpu.VMEM((d, d), BF16)],
        compiler_params=_cparams(("arbitrary",)),
        name="oproj_ln",
    )(a, w_all, b_o, x, g_all.reshape(-1, 1, d), b_all.reshape(-1, 1, d))


def _diff_prompt_kernel(lam_ref, sg_ref, q_ref, k_ref, v_ref, o_ref,
                        qt_ref, m_ref, l_ref, acc_ref, *, layer):
    tq = DIFF_TQ
    dh = DIFF_HEAD_DIM
    dv = 2 * dh
    kvh = pl.program_id(1)
    qi = pl.program_id(2)

    qt = (q_ref[...] * (dh ** -0.5)).T
    arow = lax.broadcasted_iota(jnp.int32, (dh, tq), 0)
    q_off = lax.broadcasted_iota(jnp.int32, (dh, tq), 1).astype(F32)
    for g in range(DIFF_GROUP):
        slope = jnp.where(kvh == 0, 2.0 ** -(g + 1), 2.0 ** -(DIFF_GROUP + g + 1)).astype(F32)
        extra = jnp.where(arow < 2, slope, jnp.where(arow == 2, -slope * q_off, 0.0))
        q0 = qt[(g * 2) * dh:(g * 2 + 1) * dh]
        q1 = qt[(g * 2 + 1) * dh:(g * 2 + 2) * dh]
        qt_ref[0, :, g * tq:(g + 1) * tq] = jnp.concatenate([q0, extra], axis=0).astype(BF16)
        qt_ref[1, :, g * tq:(g + 1) * tq] = jnp.concatenate([extra, q1], axis=0).astype(BF16)
    m_ref[...] = jnp.full(m_ref.shape, NEG_INF, F32)
    l_ref[...] = jnp.zeros(l_ref.shape, F32)
    acc_ref[...] = jnp.zeros(acc_ref.shape, F32)

    lane = lax.broadcasted_iota(jnp.int32, (tq, dv), 1)
    k_off = lax.broadcasted_iota(jnp.int32, (tq, dv), 0).astype(F32)
    krow = lax.broadcasted_iota(jnp.int32, (tq, tq), 0)
    qcol = lax.broadcasted_iota(jnp.int32, (tq, tq), 1)
    causal = krow <= qcol

    def block(j, masked):
        start = pl.multiple_of(j * tq, tq)
        kb = k_ref[pl.ds(start, tq), :]
        vt = v_ref[pl.ds(start, tq), :].T.astype(BF16)
        shift = ((j - qi) * tq).astype(F32)
        keys = [
            jnp.where(lane < dh, kb,
                      jnp.where(lane == dh, k_off,
                                jnp.where(lane == dh + 1, shift, jnp.where(lane == dh + 2, 1.0, 0.0)))),
            jnp.where(lane >= dh, kb,
                      jnp.where(lane == 0, k_off,
                                jnp.where(lane == 1, shift, jnp.where(lane == 2, 1.0, 0.0)))),
        ]
        for c in range(2):
            s_all = jnp.dot(keys[c].astype(BF16), qt_ref[c], preferred_element_type=F32)
            for g in range(DIFF_GROUP):
                idx = g * 2 + c
                s = s_all[:, g * tq:(g + 1) * tq]
                if masked:
                    s = jnp.where(causal, s, NEG_INF)
                m_old = m_ref[idx:idx + 1, :]
                m_new = jnp.maximum(m_old, jnp.max(s, axis=0, keepdims=True))
                alpha = jnp.exp(m_old - m_new)
                p = jnp.exp(s - m_new)
                l_ref[idx:idx + 1, :] = alpha * l_ref[idx:idx + 1, :] + jnp.sum(p, axis=0, keepdims=True)
                acc_ref[idx] = alpha * acc_ref[idx] + jnp.dot(vt, p.astype(BF16),
                                                              preferred_element_type=F32)
                m_ref[idx:idx + 1, :] = m_new

    def body(j, carry):
        block(j, False)
        return carry

    lax.fori_loop(0, qi, body, 0)
    block(qi, True)

    lp = lam_ref[...]
    lam = (jnp.exp(jnp.sum(lp[0:1] * lp[1:2], axis=-1, keepdims=True))
           - jnp.exp(jnp.sum(lp[2:3] * lp[3:4], axis=-1, keepdims=True)) + _lambda_init(layer))
    gain = sg_ref[...] * (1.0 - _lambda_init(layer))
    for g in range(DIFF_GROUP):
        o1 = acc_ref[g * 2] * (1.0 / l_ref[g * 2:g * 2 + 1, :])
        o2 = acc_ref[g * 2 + 1] * (1.0 / l_ref[g * 2 + 1:g * 2 + 2, :])
        ot = o1 - lam * o2
        ot = ot * lax.rsqrt(jnp.mean(ot * ot, axis=0, keepdims=True) + LN_EPS)
        o_ref[:, g * dv:(g + 1) * dv] = (ot.T * gain).astype(o_ref.dtype)


def diff_prompt_attention(h, diff_lam, subln_g, slot, layer, n_batch, seq):
    tq = DIFF_TQ
    nq = seq // tq
    qw = DIFF_GROUP * 2 * DIFF_HEAD_DIM
    kw = 2 * DIFF_HEAD_DIM
    return pl.pallas_call(
        functools.partial(_diff_prompt_kernel, layer=layer),
        out_shape=jax.ShapeDtypeStruct((n_batch * seq, DIFF_Q_DIM), BF16),
        grid=(n_batch, DIFF_KV_HEADS, nq),
        in_specs=[
            pl.BlockSpec((None, 4, DIFF_HEAD_DIM), lambda b, h_, i: (slot, 0, 0)),
            pl.BlockSpec((None, 1, kw), lambda b, h_, i: (slot, 0, 0)),
            pl.BlockSpec((tq, qw), lambda b, h_, i: (b * nq + i, h_)),
            pl.BlockSpec((seq, kw), lambda b, h_, i: (b, DIFF_Q_DIM // kw + h_)),
            pl.BlockSpec((seq, kw), lambda b, h_, i: (b, (DIFF_Q_DIM + DIFF_K_DIM) // kw + h_)),
        ],
        out_specs=pl.BlockSpec((tq, qw), lambda b, h_, i: (b * nq + i, h_)),
        scratch_shapes=[
            pltpu.VMEM((2, 2 * DIFF_HEAD_DIM, DIFF_GROUP * tq), BF16),
            pltpu.VMEM((2 * DIFF_GROUP, tq), F32),
            pltpu.VMEM((2 * DIFF_GROUP, tq), F32),
            pltpu.VMEM((2 * DIFF_GROUP, 2 * DIFF_HEAD_DIM, tq), F32),
        ],
        compiler_params=_cparams(("arbitrary", "arbitrary", "arbitrary")),
        name="diff_prompt_attn",
    )(diff_lam, subln_g.reshape(-1, 1, kw), h, h, h)


def _diff_decode_kernel(pt_ref, lam_ref, sg_ref, slope_ref, wq_ref, kn_ref, vn_ref, *rest,
                        layer, n_pages, past_len):
    del pt_ref
    np_ = DECODE_PAGES
    k_refs = rest[:np_]
    v_refs = rest[np_:2 * np_]
    o_ref = rest[2 * np_]
    m_ref, l_ref, acc_ref = rest[2 * np_ + 1:]
    jc = pl.program_id(1)
    n_chunks = n_pages // np_
    dv = 2 * DIFF_HEAD_DIM

    @pl.when(jc == 0)
    def _():
        m_ref[...] = jnp.full(m_ref.shape, NEG_INF, F32)
        l_ref[...] = jnp.zeros(l_ref.shape, F32)
        acc_ref[...] = jnp.zeros(acc_ref.shape, F32)

    wq = wq_ref[...]
    slope = slope_ref[...]
    lane = lax.broadcasted_iota(jnp.int32, (1, PAGE_SIZE), 1)
    rowi = lax.broadcasted_iota(jnp.int32, (16, 1), 0)
    is_kv0 = (rowi % (2 * DIFF_GROUP)) < DIFF_GROUP

    s_pages = []
    for g in range(np_):
        kt = k_refs[g][...].astype(BF16)
        s = jnp.dot(wq, kt, preferred_element_type=F32)
        k_pos = (jc * np_ + g) * PAGE_SIZE + lane
        s_pages.append(s - slope * (past_len - k_pos).astype(F32))
    m_old = m_ref[...]
    m_new = m_old
    for s in s_pages:
        m_new = jnp.maximum(m_new, jnp.max(s, axis=-1, keepdims=True))
    alpha = jnp.exp(m_old - m_new)
    l_new = alpha * l_ref[...]
    acc = alpha * acc_ref[...]
    for g in range(np_):
        p = jnp.exp(s_pages[g] - m_new)
        l_new = l_new + jnp.sum(p, axis=-1, keepdims=True)
        pb = p.astype(BF16)
        v0 = v_refs[g][pl.ds(0, PAGE_SIZE, stride=DIFF_KV_HEADS), :].astype(BF16)
        v1 = v_refs[g][pl.ds(1, PAGE_SIZE, stride=DIFF_KV_HEADS), :].astype(BF16)
        acc = acc + jnp.where(is_kv0, jnp.dot(pb, v0, preferred_element_type=F32),
                              jnp.dot(pb, v1, preferred_element_type=F32))
    m_ref[...] = m_new
    l_ref[...] = l_new
    acc_ref[...] = acc

    @pl.when(jc == n_chunks - 1)
    def _():
        kn = kn_ref[...]
        vn = vn_ref[...]
        s_n = jnp.sum(wq.astype(F32) * kn, axis=-1, keepdims=True)
        m_o = m_ref[...]
        m_f = jnp.maximum(m_o, s_n)
        a_f = jnp.exp(m_o - m_f)
        p_n = jnp.exp(s_n - m_f)
        l_f = a_f * l_ref[...] + p_n
        v_rows = jnp.where(is_kv0, vn[:, :dv], vn[:, dv:])
        acc_f = a_f * acc_ref[...] + p_n * v_rows
        o_maps = acc_f / l_f
        lp = lam_ref[...]
        lam = (jnp.exp(jnp.sum(lp[0:1] * lp[1:2], axis=-1, keepdims=True))
               - jnp.exp(jnp.sum(lp[2:3] * lp[3:4], axis=-1, keepdims=True)) + _lambda_init(layer))
        o = o_maps[:DIFF_HEADS] - lam * o_maps[DIFF_HEADS:]
        gain = sg_ref[...] * (1.0 - _lambda_init(layer))
        o = o * lax.rsqrt(jnp.mean(o * o, axis=-1, keepdims=True) + LN_EPS) * gain
        o_ref[...] = o.astype(o_ref.dtype)


def diff_decode_attention(hs, kt_cache, v_cache, page_table, diff_lam, subln_g, slot, layer):
    n_seq = hs.shape[0]
    n_pages = page_table.shape[1]
    past_len = n_pages * PAGE_SIZE
    dh = DIFF_HEAD_DIM
    q = hs[:, :DIFF_Q_DIM].reshape(n_seq, DIFF_KV_HEADS, DIFF_GROUP, 2, dh) * (dh ** -0.5)
    eye = jnp.eye(2, dtype=F32)
    wq = jnp.einsum('bkgcd,kK,cC->bckgKCd', q, eye, eye).reshape(n_seq, 16, DIFF_K_DIM).astype(BF16)
    kn = hs[:, DIFF_Q_DIM:DIFF_Q_DIM + DIFF_K_DIM].reshape(n_seq, 1, DIFF_K_DIM)
    vn = hs[:, DIFF_Q_DIM + DIFF_K_DIM:].reshape(n_seq, 1, DIFF_V_DIM)
    heads = np.arange(DIFF_HEADS, dtype=np.float32).reshape(DIFF_KV_HEADS, DIFF_GROUP)
    slopes = np.tile((2.0 ** -(heads + 1.0)).reshape(1, DIFF_HEADS), (2, 1)).reshape(16, 1)
    slopes = jnp.asarray(slopes, F32)

    np_ = DECODE_PAGES
    n_chunks = n_pages // np_

    def page_map(g):
        return lambda b, jc, pt: (slot, pt[b, jc * np_ + g], 0, 0)

    in_specs = [
        pl.BlockSpec((None, 4, dh), lambda b, jc, pt: (slot, 0, 0)),
        pl.BlockSpec((None, 1, 2 * dh), lambda b, jc, pt: (slot, 0, 0)),
        pl.BlockSpec((16, 1), lambda b, jc, pt: (0, 0)),
        pl.BlockSpec((None, 16, DIFF_K_DIM), lambda b, jc, pt: (b, 0, 0)),
        pl.BlockSpec((None, 1, DIFF_K_DIM), lambda b, jc, pt: (b, 0, 0)),
        pl.BlockSpec((None, 1, DIFF_V_DIM), lambda b, jc, pt: (b, 0, 0)),
    ]
    in_specs += [pl.BlockSpec((None, None, DIFF_K_DIM, PAGE_SIZE), page_map(g)) for g in range(np_)]
    in_specs += [pl.BlockSpec((None, None, PAGE_SIZE * DIFF_KV_HEADS, 2 * dh), page_map(g))
                 for g in range(np_)]
    out = pl.pallas_call(
        functools.partial(_diff_decode_kernel, layer=layer, n_pages=n_pages, past_len=past_len),
        out_shape=jax.ShapeDtypeStruct((n_seq, DIFF_HEADS, 2 * dh), BF16),
        grid_spec=pltpu.PrefetchScalarGridSpec(
            num_scalar_prefetch=1,
            grid=(n_seq, n_chunks),
            in_specs=in_specs,
            out_specs=pl.BlockSpec((None, DIFF_HEADS, 2 * dh), lambda b, jc, pt: (b, 0, 0)),
            scratch_shapes=[
                pltpu.VMEM((16, 1), F32),
                pltpu.VMEM((16, 1), F32),
                pltpu.VMEM((16, 2 * dh), F32),
            ],
        ),
        compiler_params=_cparams(("arbitrary", "arbitrary")),
        name="diff_decode_attn",
    )(page_table, diff_lam, subln_g.reshape(-1, 1, 2 * dh), slopes, wq, kn, vn,
      *([kt_cache] * np_), *([v_cache] * np_))
    return out.reshape(n_seq, DIFF_Q_DIM)


def _swa_slopes():
    return 2.0 ** (-8.0 * np.arange(1, SWA_HEADS + 1, dtype=np.float32) / SWA_HEADS)


def _swa_prompt_kernel(sink_ref, slope_ref, q_ref, cur_ref, prev_ref, o_ref, *, n_blocks):
    w = WINDOW
    dh = SWA_HEAD_DIM
    bi = pl.program_id(0) % n_blocks
    krow = lax.broadcasted_iota(jnp.int32, (2 * w, w), 0)
    qcol = lax.broadcasted_iota(jnp.int32, (2 * w, w), 1)
    dist = (qcol - krow + w).astype(F32)
    in_seq = jnp.logical_or(krow >= w, (jnp.zeros_like(krow) + bi) > 0)
    ok = jnp.logical_and(jnp.logical_and(krow > qcol, krow <= qcol + w), in_seq)

    qt = (q_ref[...] * (dh ** -0.5)).T.astype(BF16)
    kk = jnp.concatenate([prev_ref[:, :SWA_KV_DIM], cur_ref[:, :SWA_KV_DIM]], axis=0).astype(BF16)
    vt = jnp.concatenate([prev_ref[:, SWA_KV_DIM:], cur_ref[:, SWA_KV_DIM:]], axis=0).T.astype(BF16)
    for kh in range(SWA_KV_HEADS):
        q_heads = jnp.concatenate(
            [qt[(kh * SWA_GROUP + g) * dh:(kh * SWA_GROUP + g + 1) * dh] for g in range(SWA_GROUP)],
            axis=1)
        s_all = jnp.dot(kk[:, kh * dh:(kh + 1) * dh], q_heads, preferred_element_type=F32)
        probs = []
        for g in range(SWA_GROUP):
            hd = kh * SWA_GROUP + g
            sink = sink_ref[hd]
            s = jnp.where(ok, s_all[:, g * w:(g + 1) * w] - slope_ref[hd] * dist, NEG_INF)
            m = jnp.maximum(jnp.max(s, axis=0, keepdims=True), sink)
            e = jnp.exp(s - m)
            den = jnp.sum(e, axis=0, keepdims=True) + jnp.exp(sink - m)
            probs.append((e * (1.0 / den)).astype(BF16))
        ot = jnp.dot(vt[kh * dh:(kh + 1) * dh], jnp.concatenate(probs, axis=1),
                     preferred_element_type=F32)
        for g in range(0, SWA_GROUP, 2):
            pair = jnp.concatenate([ot[:, g * w:(g + 1) * w], ot[:, (g + 1) * w:(g + 2) * w]], axis=0)
            hd = kh * SWA_GROUP + g
            o_ref[:, hd * dh:(hd + 2) * dh] = pair.T.astype(o_ref.dtype)


def swa_prompt_attention(h, sinks, n_batch, seq):
    nb = seq // WINDOW
    kvw = 2 * SWA_KV_DIM
    kv_col = SWA_Q_DIM // kvw
    slopes = jnp.asarray(_swa_slopes(), F32)
    smem = pl.BlockSpec(memory_space=pltpu.SMEM)
    return pl.pallas_call(
        functools.partial(_swa_prompt_kernel, n_blocks=nb),
        out_shape=jax.ShapeDtypeStruct((n_batch * seq, SWA_Q_DIM), BF16),
        grid=(n_batch * nb,),
        in_specs=[
            smem, smem,
            pl.BlockSpec((WINDOW, SWA_Q_DIM), lambda r: (r, 0)),
            pl.BlockSpec((WINDOW, kvw), lambda r: (r, kv_col)),
            pl.BlockSpec((WINDOW, kvw), lambda r: (jnp.maximum(r - 1, 0), kv_col)),
        ],
        out_specs=pl.BlockSpec((WINDOW, SWA_Q_DIM), lambda r: (r, 0)),
        compiler_params=_cparams(("arbitrary",)),
        name="swa_prompt_attn",
    )(sinks, slopes, h, h, h)


def _swa_decode_kernel(sink_ref, slope_ref, wq_ref, kn_ref, vn_ref, kt_ref, vt_ref, o_ref, *, win_buf):
    dh = SWA_HEAD_DIM
    lane = lax.broadcasted_iota(jnp.int32, (1, win_buf), 1)
    dist = (win_buf - lane).astype(F32)
    ok = dist < float(WINDOW)
    rowi = lax.broadcasted_iota(jnp.int32, (SWA_HEADS, 1), 0)
    is_kv0 = rowi < SWA_GROUP
    sink = sink_ref[...]
    slope = slope_ref[...]
    for sq in range(SWA_SEQS):
        wq = wq_ref[sq]
        kt = kt_ref[sq].astype(BF16)
        vt = vt_ref[sq].astype(BF16)
        s = jnp.dot(wq, kt, preferred_element_type=F32)
        s = jnp.where(ok, s - slope * dist, NEG_INF)
        s_n = jnp.sum(wq.astype(F32) * kn_ref[sq], axis=-1, keepdims=True)
        m = jnp.maximum(jnp.maximum(jnp.max(s, axis=-1, keepdims=True), s_n), sink)
        e = jnp.exp(s - m)
        e_n = jnp.exp(s_n - m)
        den = jnp.sum(e, axis=-1, keepdims=True) + e_n + jnp.exp(sink - m)
        inv = 1.0 / den
        pv = lax.dot_general((e * inv).astype(BF16), vt, (((1,), (1,)), ((), ())),
                             preferred_element_type=F32)
        pv = pv + (e_n * inv) * vn_ref[sq]
        o_ref[sq] = jnp.where(is_kv0, pv[:, :dh], pv[:, dh:]).astype(o_ref.dtype)


def swa_decode_attention(hs, kt_buf, vt_buf, sinks):
    n_seq = hs.shape[0]
    win_buf = kt_buf.shape[-1]
    dh = SWA_HEAD_DIM
    q = hs[:, :SWA_Q_DIM].reshape(n_seq, SWA_KV_HEADS, SWA_GROUP, dh) * (dh ** -0.5)
    eye = jnp.eye(SWA_KV_HEADS, dtype=F32)
    wq = jnp.einsum('bkgd,kK->bkgKd', q, eye).reshape(n_seq, SWA_HEADS, SWA_KV_DIM).astype(BF16)
    kn = hs[:, SWA_Q_DIM:SWA_Q_DIM + SWA_KV_DIM].reshape(n_seq, 1, SWA_KV_DIM)
    vn = hs[:, SWA_Q_DIM + SWA_KV_DIM:].reshape(n_seq, 1, SWA_KV_DIM)
    slopes = jnp.asarray(_swa_slopes().reshape(SWA_HEADS, 1), F32)
    sb = SWA_SEQS
    seq3 = lambda i: (i, 0, 0)
    out = pl.pallas_call(
        functools.partial(_swa_decode_kernel, win_buf=win_buf),
        out_shape=jax.ShapeDtypeStruct((n_seq, SWA_HEADS, dh), BF16),
        grid=(n_seq // sb,),
        in_specs=[
            pl.BlockSpec((SWA_HEADS, 1), lambda i: (0, 0)),
            pl.BlockSpec((SWA_HEADS, 1), lambda i: (0, 0)),
            pl.BlockSpec((sb, SWA_HEADS, SWA_KV_DIM), seq3),
            pl.BlockSpec((sb, 1, SWA_KV_DIM), seq3),
            pl.BlockSpec((sb, 1, SWA_KV_DIM), seq3),
            pl.BlockSpec((sb, SWA_KV_DIM, win_buf), seq3),
            pl.BlockSpec((sb, SWA_KV_DIM, win_buf), seq3),
        ],
        out_specs=pl.BlockSpec((sb, SWA_HEADS, dh), seq3),
        compiler_params=_cparams(("arbitrary",)),
        name="swa_decode_attn",
    )(sinks.reshape(SWA_HEADS, 1), slopes, wq, kn, vn, kt_buf, vt_buf)
    return out.reshape(n_seq, SWA_Q_DIM)


def _router_kernel(x_ref, w_ref, b_ref, e_ref, g_ref, r_ref, cnt_ref, carry_ref):
    tm = x_ref.shape[0]

    @pl.when(pl.program_id(0) == 0)
    def _():
        carry_ref[...] = jnp.zeros(carry_ref.shape, F32)

    logits = jnp.dot(x_ref[...].astype(BF16), w_ref[...].astype(BF16),
                     preferred_element_type=F32) + b_ref[...]
    lane = lax.broadcasted_iota(jnp.int32, (tm, N_EXPERTS), 1).astype(F32)
    work = logits
    vals, idxs = [], []
    for _ in range(TOP_K):
        mx = jnp.max(work, axis=-1, keepdims=True)
        ix = jnp.min(jnp.where(work == mx, lane, float(N_EXPERTS)), axis=-1, keepdims=True)
        vals.append(mx)
        idxs.append(ix)
        work = jnp.where(lane == ix, -jnp.inf, work)
    ex = [jnp.exp(v - vals[0]) for v in vals]
    den = ex[0] + ex[1] + ex[2] + ex[3]

    onehot = jnp.zeros((tm, N_EXPERTS), F32)
    for ix in idxs:
        onehot = onehot + (lane == ix).astype(F32)
    ri = lax.broadcasted_iota(jnp.int32, (tm, tm), 0)
    ci = lax.broadcasted_iota(jnp.int32, (tm, tm), 1)
    tri = (ci < ri).astype(BF16)
    before = jnp.dot(tri, onehot.astype(BF16), preferred_element_type=F32) + carry_ref[...]

    col4 = lax.broadcasted_iota(jnp.int32, (tm, TOP_K), 1)
    e_out = jnp.zeros((tm, TOP_K), jnp.int32)
    g_out = jnp.zeros((tm, TOP_K), F32)
    r_out = jnp.zeros((tm, TOP_K), jnp.int32)
    for k in range(TOP_K):
        rank = jnp.sum(jnp.where(lane == idxs[k], before, 0.0), axis=-1, keepdims=True)
        e_out = jnp.where(col4 == k, idxs[k].astype(jnp.int32), e_out)
        g_out = jnp.where(col4 == k, ex[k] / den, g_out)
        r_out = jnp.where(col4 == k, rank.astype(jnp.int32), r_out)
    e_ref[...] = e_out
    g_ref[...] = g_out
    r_ref[...] = r_out
    carry_ref[...] = carry_ref[...] + jnp.sum(onehot, axis=0, keepdims=True)
    cnt_ref[...] = carry_ref[...]


def router(x, w_router, b_router, layer):
    t, d = x.shape
    row = lambda i: (i, 0)
    return pl.pallas_call(
        _router_kernel,
        out_shape=(
            jax.ShapeDtypeStruct((t, TOP_K), jnp.int32),
            jax.ShapeDtypeStruct((t, TOP_K), F32),
            jax.ShapeDtypeStruct((t, TOP_K), jnp.int32),
            jax.ShapeDtypeStruct((1, N_EXPERTS), F32),
        ),
        grid=(t // ROW_TILE,),
        in_specs=[
            pl.BlockSpec((ROW_TILE, d), row),
            pl.BlockSpec((None, d, N_EXPERTS), lambda i: (layer, 0, 0)),
            pl.BlockSpec((None, 1, N_EXPERTS), lambda i: (layer, 0, 0)),
        ],
        out_specs=(
            pl.BlockSpec((ROW_TILE, TOP_K), row),
            pl.BlockSpec((ROW_TILE, TOP_K), row),
            pl.BlockSpec((ROW_TILE, TOP_K), row),
            pl.BlockSpec((1, N_EXPERTS), lambda i: (0, 0)),
        ),
        scratch_shapes=[pltpu.VMEM((1, N_EXPERTS), F32)],
        compiler_params=_cparams(("arbitrary",)),
        name="router",
    )(x, w_router, b_router.reshape(-1, 1, N_EXPERTS))


MAX_COPIES_PER_STEP = 2048


def _copies_per_step(n):
    return max(c for c in range(8, MAX_COPIES_PER_STEP + 1, 8) if n % c == 0)


def _token_copy(src_ref, dst_ref, sem, src_tok, dst_tok):
    return pltpu.make_async_copy(src_ref.at[src_tok], dst_ref.at[dst_tok], sem)


def _drain_token_copies(src_ref, dst_ref, sem, n):
    def drain(i, carry):
        _token_copy(src_ref, dst_ref, sem, 0, 0).wait()
        return carry

    lax.fori_loop(0, n, drain, 0, unroll=8)


def _scatter_tokens_kernel(idx_ref, src_ref, out_ref, sem, *, n):
    def issue(i, carry):
        tok = lax.shift_right_logical(i, TOP_K.bit_length() - 1)
        _token_copy(src_ref, out_ref, sem, tok, idx_ref[0, i]).start()
        return carry

    lax.fori_loop(0, n, issue, 0, unroll=8)
    _drain_token_copies(src_ref, out_ref, sem, n)


def scatter_tokens(src_tiles, idx, n_out):
    t = src_tiles.shape[0] // TILE_ROWS
    n = _copies_per_step(idx.shape[0])
    steps = idx.shape[0] // n
    out = pl.pallas_call(
        functools.partial(_scatter_tokens_kernel, n=n),
        out_shape=jax.ShapeDtypeStruct((n_out, TILE_ROWS, 128), F32),
        grid=(steps,),
        in_specs=[pl.BlockSpec((None, 1, n), lambda s: (s, 0, 0), memory_space=pltpu.SMEM),
                  pl.BlockSpec((n // TOP_K, TILE_ROWS, 128), lambda s: (s, 0, 0))],
        out_specs=pl.BlockSpec(memory_space=pl.ANY),
        scratch_shapes=[pltpu.SemaphoreType.DMA(())],
        compiler_params=_cparams(("arbitrary",)),
        name="scatter_tokens",
    )(idx.reshape(steps, 1, n), src_tiles.reshape(t, TILE_ROWS, 128))
    return out.reshape(n_out * TILE_ROWS, 128)


def _gather_tokens_kernel(idx_ref, src_ref, out_ref, sem, *, n):
    def issue(i, carry):
        _token_copy(src_ref, out_ref, sem, idx_ref[0, i], i).start()
        return carry

    lax.fori_loop(0, n, issue, 0, unroll=8)
    _drain_token_copies(src_ref, out_ref, sem, n)


def gather_tokens(src_tiles, idx):
    r = src_tiles.shape[0] // TILE_ROWS
    n_idx = idx.shape[0]
    n = _copies_per_step(n_idx)
    steps = n_idx // n
    out = pl.pallas_call(
        functools.partial(_gather_tokens_kernel, n=n),
        out_shape=jax.ShapeDtypeStruct((n_idx, TILE_ROWS, 128), F32),
        grid=(steps,),
        in_specs=[pl.BlockSpec((None, 1, n), lambda s: (s, 0, 0), memory_space=pltpu.SMEM),
                  pl.BlockSpec(memory_space=pl.ANY)],
        out_specs=pl.BlockSpec((n, TILE_ROWS, 128), lambda s: (s, 0, 0)),
        scratch_shapes=[pltpu.SemaphoreType.DMA(())],
        compiler_params=_cparams(("arbitrary",)),
        name="gather_tokens",
    )(idx.reshape(steps, 1, n), src_tiles.reshape(r, TILE_ROWS, 128))
    return out.reshape(n_idx * TILE_ROWS, 128)


def _experts_kernel(be_ref, nx_ref, bv_ref, x_ref, wgu_hbm, bgu_ref, wd_hbm, bd_ref, o_ref,
                    wgu_f32, wd_f32, wgu_bf, wd_bf, sems, *, layer):
    s = pl.program_id(0)
    e = be_ref[s]
    valid = bv_ref[s]
    new_expert = jnp.logical_or(s == 0, e != be_ref[jnp.maximum(s - 1, 0)])
    used = valid > 0

    def weight_copies(expert):
        return (pltpu.make_async_copy(wgu_hbm.at[layer, expert], wgu_f32, sems.at[0]),
                pltpu.make_async_copy(wd_hbm.at[layer, expert], wd_f32, sems.at[1]))

    @pl.when(s == 0)
    def _():
        for cp in weight_copies(e):
            cp.start()

    @pl.when(jnp.logical_and(new_expert, used))
    def _():
        for cp in weight_copies(e):
            cp.wait()
        wgu_bf[...] = wgu_f32[...].astype(BF16)
        wd_bf[...] = wd_f32[...].astype(BF16)

        @pl.when(nx_ref[s] >= 0)
        def _():
            for cp in weight_copies(nx_ref[s]):
                cp.start()

    @pl.when(used)
    def _():
        rows = lax.broadcasted_iota(jnp.int32, (MOE_TM, 1), 0)
        x = jnp.where(rows < valid, _load_token_tiles(x_ref), 0.0).astype(BF16)
        h = jnp.dot(x, wgu_bf[...], preferred_element_type=F32) + bgu_ref[...]
        gate = jnp.minimum(h[:, :D_FF], SWIGLU_LIMIT)
        up = jnp.clip(h[:, D_FF:], -SWIGLU_LIMIT, SWIGLU_LIMIT)
        act = (up + 1.0) * (gate * jax.nn.sigmoid(SWIGLU_ALPHA * gate))
        y = jnp.dot(act.astype(BF16), wd_bf[...], preferred_element_type=F32) + bd_ref[...]
        _store_token_tiles(o_ref, y)

    @pl.when(jnp.logical_not(used))
    def _():
        o_ref[...] = jnp.zeros(o_ref.shape, o_ref.dtype)


def experts(xs, block_e, next_e, block_valid, w_gu, b_gu, w_down, b_down, layer):
    d = D_MODEL
    n_blocks = xs.shape[0] // (MOE_TM * TILE_ROWS)
    tile_block = pl.BlockSpec((MOE_TM * TILE_ROWS, 128), lambda s, be, nx, bv: (s, 0))
    any_spec = pl.BlockSpec(memory_space=pl.ANY)
    return pl.pallas_call(
        functools.partial(_experts_kernel, layer=layer),
        out_shape=jax.ShapeDtypeStruct(xs.shape, F32),
        grid_spec=pltpu.PrefetchScalarGridSpec(
            num_scalar_prefetch=3,
            grid=(n_blocks,),
            in_specs=[
                tile_block,
                any_spec,
                pl.BlockSpec((None, None, 1, 2 * D_FF), lambda s, be, nx, bv: (layer, be[s], 0, 0)),
                any_spec,
                pl.BlockSpec((None, None, 1, d), lambda s, be, nx, bv: (layer, be[s], 0, 0)),
            ],
            out_specs=tile_block,
            scratch_shapes=[
                pltpu.VMEM((d, 2 * D_FF), F32),
                pltpu.VMEM((D_FF, d), F32),
                pltpu.VMEM((d, 2 * D_FF), BF16),
                pltpu.VMEM((D_FF, d), BF16),
                pltpu.SemaphoreType.DMA((2,)),
            ],
        ),
        compiler_params=_cparams(("arbitrary",)),
        name="experts",
    )(block_e, next_e, block_valid, xs, w_gu, b_gu.reshape(DEPTH, N_EXPERTS, 1, 2 * D_FF),
      w_down, b_down.reshape(DEPTH, N_EXPERTS, 1, d))


def _combine_ln_kernel(x_ref, y0_ref, y1_ref, y2_ref, y3_ref, gt_ref, g_ref, b_ref, o_ref):
    gt = gt_ref[...]
    f = _load_token_tiles(y0_ref) * gt[:, 0:1]
    for k, y_ref in enumerate((y1_ref, y2_ref, y3_ref), start=1):
        f = f + _load_token_tiles(y_ref) * gt[:, k:k + 1]
    o_ref[...] = _layer_norm_rows(ALPHA * x_ref[...] + f, g_ref[...], b_ref[...])


def combine_ln(x, y4, gates, g_all, b_all, layer):
    t, d = x.shape
    nt = t // ROW_TILE
    row = lambda i: (i, 0)

    def y_spec(k):
        return pl.BlockSpec((ROW_TILE * TILE_ROWS, 128), lambda i: (k * nt + i, 0))

    return pl.pallas_call(
        _combine_ln_kernel,
        out_shape=jax.ShapeDtypeStruct((t, d), F32),
        grid=(nt,),
        in_specs=[
            pl.BlockSpec((ROW_TILE, d), row),
            y_spec(0), y_spec(1), y_spec(2), y_spec(3),
            pl.BlockSpec((ROW_TILE, TOP_K), row),
            pl.BlockSpec((None, 1, d), lambda i: (layer, 0, 0)),
            pl.BlockSpec((None, 1, d), lambda i: (layer, 0, 0)),
        ],
        out_specs=pl.BlockSpec((ROW_TILE, d), row),
        compiler_params=_cparams(("arbitrary",)),
        name="combine_ln",
    )(x, y4, y4, y4, y4, gates, g_all.reshape(-1, 1, d), b_all.reshape(-1, 1, d))


def moe_layer(x, x_tiles, layer, w_router, b_router, w_gu, b_gu, w_down, b_down, ln_g, ln_b):
    t, d = x.shape
    top_e, gates, rank, counts = router(x, w_router, b_router, layer)
    counts = counts.reshape(N_EXPERTS).astype(jnp.int32)
    padded = (counts + MOE_TM - 1) // MOE_TM * MOE_TM
    pad_end = jnp.cumsum(padded)
    pad_start = pad_end - padded
    n_blocks = -(-t * TOP_K // MOE_TM) + N_EXPERTS
    dest = pad_start[top_e] + rank
    block_start = jnp.arange(n_blocks, dtype=jnp.int32) * MOE_TM
    block_e = jnp.minimum(jnp.sum(pad_end[None, :] <= block_start[:, None], axis=1),
                          N_EXPERTS - 1).astype(jnp.int32)
    n_used = pad_end[-1] // MOE_TM
    block_valid = jnp.where(block_start < pad_end[-1],
                            jnp.clip(pad_start[block_e] + counts[block_e] - block_start, 0, MOE_TM),
                            0).astype(jnp.int32)
    after = pad_end[block_e] // MOE_TM
    next_e = jnp.where(after < n_used, block_e[jnp.minimum(after, n_blocks - 1)], -1).astype(jnp.int32)
    xs = scatter_tokens(x_tiles, dest.reshape(-1), n_blocks * MOE_TM)
    ys = experts(xs, block_e, next_e, block_valid, w_gu, b_gu, w_down, b_down, layer)
    y4 = gather_tokens(ys, dest.T.reshape(-1))
    return combine_ln(x, y4, gates, ln_g, ln_b, layer)


def kernel(x_prompt, x_sample, cache_k_diff, cache_v_diff, cache_k_swa, cache_v_swa, page_table,
           ln1_g, ln1_b, ln2_g, ln2_b, w_qkv_diff, diff_lam, diff_subln_g, w_o_diff,
           w_qkv_swa, b_qkv_swa, swa_sinks, w_o_swa, b_o_swa,
           w_router, b_router, w_gu, b_gu, w_down, b_down):
    n_batch, seq, d = x_prompt.shape
    n_seq = x_sample.shape[0]
    tp = n_batch * seq
    x = jnp.concatenate([x_prompt.reshape(tp, d), x_sample.reshape(n_seq, d)], axis=0)

    n_layers_diff, n_phys = cache_k_diff.shape[:2]
    kt_cache = jnp.transpose(cache_k_diff, (0, 1, 3, 4, 5, 2)).reshape(
        n_layers_diff, n_phys, DIFF_K_DIM, PAGE_SIZE)
    v_cache = cache_v_diff.reshape(n_layers_diff, n_phys, PAGE_SIZE * DIFF_KV_HEADS, 2 * DIFF_HEAD_DIM)
    win_buf = cache_k_swa.shape[2]
    kt_swa = jnp.transpose(cache_k_swa, (0, 1, 3, 4, 2)).reshape(-1, n_seq, SWA_KV_DIM, win_buf)
    vt_swa = jnp.transpose(cache_v_swa, (0, 1, 3, 4, 2)).reshape(-1, n_seq, SWA_KV_DIM, win_buf)

    zero_bias_qkv = jnp.zeros((1, w_qkv_diff.shape[-1]), F32)
    zero_bias_o = jnp.zeros((1, d), F32)

    kdp, vdp, kds, vds = [], [], [], []
    ksp, vsp, kss, vss = [], [], [], []
    for i in range(DEPTH):
        j = i // 2
        if i % 2 == 0:
            h = linear(x, w_qkv_diff, j, zero_bias_qkv, 512)
            hs = h[tp:]
            a_p = diff_prompt_attention(h, diff_lam, diff_subln_g, j, i, n_batch, seq)
            a_s = diff_decode_attention(hs, kt_cache, v_cache, page_table, diff_lam, diff_subln_g, j, i)
            kq, vq = DIFF_Q_DIM, DIFF_Q_DIM + DIFF_K_DIM
            kdp.append(h[:tp, kq:vq].reshape(n_batch, seq, DIFF_KV_HEADS, 2, DIFF_HEAD_DIM))
            vdp.append(h[:tp, vq:].reshape(n_batch, seq, DIFF_KV_HEADS, 2 * DIFF_HEAD_DIM))
            kds.append(hs[:, kq:vq].reshape(n_seq, 1, DIFF_KV_HEADS, 2, DIFF_HEAD_DIM))
            vds.append(hs[:, vq:].reshape(n_seq, 1, DIFF_KV_HEADS, 2 * DIFF_HEAD_DIM))
            a = jnp.concatenate([a_p, a_s], axis=0)
            x, x_tiles = oproj_ln(a, w_o_diff, j, zero_bias_o, x, ln1_g, ln1_b, i)
        else:
            h = linear(x, w_qkv_swa, j, b_qkv_swa[j].reshape(1, -1), 640)
            hs = h[tp:]
            a_p = swa_prompt_attention(h, swa_sinks[j], n_batch, seq)
            a_s = swa_decode_attention(hs, kt_swa[j], vt_swa[j], swa_sinks[j])
            kq, vq = SWA_Q_DIM, SWA_Q_DIM + SWA_KV_DIM
            keep = min(WINDOW, seq)
            hp = h[:tp].reshape(n_batch, seq, -1)[:, seq - keep:]
            ksp.append(hp[..., kq:vq].reshape(n_batch, keep, SWA_KV_HEADS, SWA_HEAD_DIM))
            vsp.append(hp[..., vq:].reshape(n_batch, keep, SWA_KV_HEADS, SWA_HEAD_DIM))
            k_new = hs[:, kq:vq].reshape(n_seq, 1, SWA_KV_HEADS, SWA_HEAD_DIM)
            v_new = hs[:, vq:].reshape(n_seq, 1, SWA_KV_HEADS, SWA_HEAD_DIM)
            kss.append(jnp.concatenate([cache_k_swa[j], k_new], axis=1)[:, -win_buf:])
            vss.append(jnp.concatenate([cache_v_swa[j], v_new], axis=1)[:, -win_buf:])
            a = jnp.concatenate([a_p, a_s], axis=0)
            x, x_tiles = oproj_ln(a, w_o_swa, j, b_o_swa[j].reshape(1, -1), x, ln1_g, ln1_b, i)
        x = moe_layer(x, x_tiles, i, w_router, b_router, w_gu, b_gu, w_down, b_down, ln2_g, ln2_b)

    return (x[:tp].reshape(n_batch, seq, d), x[tp:].reshape(n_seq, 1, d),
            jnp.stack(kdp), jnp.stack(vdp), jnp.stack(kds), jnp.stack(vds),
            jnp.stack(ksp), jnp.stack(vsp), jnp.stack(kss), jnp.stack(vss))
```

```python
import functools
import math

import jax
import jax.numpy as jnp
import numpy as np
from jax import lax
from jax.experimental import pallas as pl
from jax.experimental.pallas import tpu as pltpu

F32 = jnp.float32
BF16 = jnp.bfloat16

D_MODEL = 1024
DEPTH = 4
PAGE_SIZE = 128
DIFF_HEAD_DIM = 64
DIFF_HEADS = 8
DIFF_KV_HEADS = 2
DIFF_GROUP = 4
DIFF_Q_DIM = 1024
DIFF_K_DIM = 256
DIFF_V_DIM = 256
SWA_HEAD_DIM = 64
SWA_HEADS = 16
SWA_KV_HEADS = 2
SWA_GROUP = 8
SWA_Q_DIM = 1024
SWA_KV_DIM = 128
WINDOW = 128
N_EXPERTS = 32
TOP_K = 4
D_FF = 1024
SWIGLU_ALPHA = 1.702
SWIGLU_LIMIT = 7.0
ALPHA = (2 * DEPTH) ** 0.25
LN_EPS = 1e-5
NEG_INF = -1e30

ROW_TILE = 384
DIFF_TQ = 256
MOE_TM = 512
DECODE_PAGES = 16
SWA_SEQS = 8
VMEM_LIMIT = 56 * 1024 * 1024


def _lambda_init(layer):
    return 0.8 - 0.6 * math.exp(-0.3 * layer)


def _cparams(sem):
    return pltpu.CompilerParams(dimension_semantics=sem, vmem_limit_bytes=VMEM_LIMIT)


def _linear_kernel(x_ref, w_ref, b_ref, o_ref, wbf_ref):
    @pl.when(pl.program_id(1) == 0)
    def _():
        wbf_ref[...] = w_ref[...].astype(BF16)

    acc = jnp.dot(x_ref[...].astype(BF16), wbf_ref[...], preferred_element_type=F32)
    o_ref[...] = acc + b_ref[...]


def linear(x, w_all, layer, bias, tn):
    t, k = x.shape
    n = w_all.shape[-1]
    return pl.pallas_call(
        _linear_kernel,
        out_shape=jax.ShapeDtypeStruct((t, n), F32),
        grid=(n // tn, t // ROW_TILE),
        in_specs=[
            pl.BlockSpec((ROW_TILE, k), lambda j, i: (i, 0)),
            pl.BlockSpec((None, k, tn), lambda j, i: (layer, 0, j)),
            pl.BlockSpec((1, tn), lambda j, i: (0, j)),
        ],
        out_specs=pl.BlockSpec((ROW_TILE, tn), lambda j, i: (i, j)),
        scratch_shapes=[pltpu.VMEM((k, tn), BF16)],
        compiler_params=_cparams(("arbitrary", "arbitrary")),
        name="linear",
    )(x, w_all, bias)


def _layer_norm_rows(z, g, b):
    mu = jnp.mean(z, axis=-1, keepdims=True)
    zc = z - mu
    var = jnp.mean(zc * zc, axis=-1, keepdims=True)
    return zc * lax.rsqrt(var + LN_EPS) * g + b


TILE_ROWS = D_MODEL // 128


def _store_token_tiles(ref, val):
    n = val.shape[0]
    for c in range(TILE_ROWS):
        ref[pl.ds(c, n, stride=TILE_ROWS), :] = val[:, c * 128:(c + 1) * 128]


def _load_token_tiles(ref):
    n = ref.shape[0] // TILE_ROWS
    return jnp.concatenate([ref[pl.ds(c, n, stride=TILE_ROWS), :] for c in range(TILE_ROWS)], axis=-1)


def _oproj_ln_kernel(a_ref, w_ref, bo_ref, x_ref, g_ref, b_ref, o_ref, ot_ref, wbf_ref):
    @pl.when(pl.program_id(0) == 0)
    def _():
        wbf_ref[...] = w_ref[...].astype(BF16)

    y = jnp.dot(a_ref[...], wbf_ref[...], preferred_element_type=F32) + bo_ref[...]
    z = _layer_norm_rows(ALPHA * x_ref[...] + y, g_ref[...], b_ref[...])
    o_ref[...] = z
    _store_token_tiles(ot_ref, z)


def oproj_ln(a, w_all, layer_slot, b_o, x, g_all, b_all, layer):
    t, d = x.shape
    row = lambda i: (i, 0)
    return pl.pallas_call(
        _oproj_ln_kernel,
        out_shape=(jax.ShapeDtypeStruct((t, d), F32),
                   jax.ShapeDtypeStruct((t * TILE_ROWS, 128), F32)),
        grid=(t // ROW_TILE,),
        in_specs=[
            pl.BlockSpec((ROW_TILE, d), row),
            pl.BlockSpec((None, d, d), lambda i: (layer_slot, 0, 0)),
            pl.BlockSpec((1, d), lambda i: (0, 0)),
            pl.BlockSpec((ROW_TILE, d), row),
            pl.BlockSpec((None, 1, d), lambda i: (layer, 0, 0)),
            pl.BlockSpec((None, 1, d), lambda i: (layer, 0, 0)),
        ],
        out_specs=(pl.BlockSpec((ROW_TILE, d), row),
                   pl.BlockSpec((ROW_TILE * TILE_ROWS, 128), row)),
        scratch_shapes=[pltpu.VMEM((d, d), BF16)],
        compiler_params=_cparams(("arbitrary",)),
        name="oproj_ln",
    )(a, w_all, b_o, x, g_all.reshape(-1, 1, d), b_all.reshape(-1, 1, d))


def _diff_prompt_kernel(lam_ref, sg_ref, q_ref, k_ref, v_ref, o_ref,
                        qt_ref, m_ref, l_ref, acc_ref, *, layer):
    tq = DIFF_TQ
    dh = DIFF_HEAD_DIM
    dv = 2 * dh
    kvh = pl.program_id(1)
    qi = pl.program_id(2)

    qt = (q_ref[...] * (dh ** -0.5)).T
    arow = lax.broadcasted_iota(jnp.int32, (dh, tq), 0)
    q_off = lax.broadcasted_iota(jnp.int32, (dh, tq), 1).astype(F32)
    for g in range(DIFF_GROUP):
        slope = jnp.where(kvh == 0, 2.0 ** -(g + 1), 2.0 ** -(DIFF_GROUP + g + 1)).astype(F32)
        extra = jnp.where(arow < 2, slope, jnp.where(arow == 2, -slope * q_off, 0.0))
        q0 = qt[(g * 2) * dh:(g * 2 + 1) * dh]
        q1 = qt[(g * 2 + 1) * dh:(g * 2 + 2) * dh]
        qt_ref[0, :, g * tq:(g + 1) * tq] = jnp.concatenate([q0, extra], axis=0).astype(BF16)
        qt_ref[1, :, g * tq:(g + 1) * tq] = jnp.concatenate([extra, q1], axis=0).astype(BF16)
    m_ref[...] = jnp.full(m_ref.shape, NEG_INF, F32)
    l_ref[...] = jnp.zeros(l_ref.shape, F32)
    acc_ref[...] = jnp.zeros(acc_ref.shape, F32)

    krow = lax.broadcasted_iota(jnp.int32, (tq, tq), 0)
    qcol = lax.broadcasted_iota(jnp.int32, (tq, tq), 1)
    causal = krow <= qcol

    def block(j, nk, masked):
        start = pl.multiple_of(j * tq, tq)
        kb = k_ref[pl.ds(start, nk), :]
        vt = v_ref[pl.ds(start, nk), :].T.astype(BF16)
        lane = lax.broadcasted_iota(jnp.int32, (nk, dv), 1)
        row = lax.broadcasted_iota(jnp.int32, (nk, dv), 0)
        k_off = jnp.bitwise_and(row, tq - 1).astype(F32)
        shift = ((j - qi) * tq).astype(F32) + jnp.bitwise_and(row, -tq).astype(F32)
        keys = [
            jnp.where(lane < dh, kb,
                      jnp.where(lane == dh, k_off,
                                jnp.where(lane == dh + 1, shift, jnp.where(lane == dh + 2, 1.0, 0.0)))),
            jnp.where(lane >= dh, kb,
                      jnp.where(lane == 0, k_off,
                                jnp.where(lane == 1, shift, jnp.where(lane == 2, 1.0, 0.0)))),
        ]
        for c in range(2):
            s_all = jnp.dot(keys[c].astype(BF16), qt_ref[c], preferred_element_type=F32)
            for g in range(DIFF_GROUP):
                idx = g * 2 + c
                s = s_all[:, g * tq:(g + 1) * tq]
                if masked:
                    s = jnp.where(causal, s, NEG_INF)
                m_old = m_ref[idx:idx + 1, :]
                m_new = jnp.maximum(m_old, jnp.max(s, axis=0, keepdims=True))
                alpha = jnp.exp(m_old - m_new)
                p = jnp.exp(s - m_new)
                l_ref[idx:idx + 1, :] = alpha * l_ref[idx:idx + 1, :] + jnp.sum(p, axis=0, keepdims=True)
                acc_ref[idx] = alpha * acc_ref[idx] + jnp.dot(vt, p.astype(BF16),
                                                              preferred_element_type=F32)
                m_ref[idx:idx + 1, :] = m_new

    def pair_body(p, carry):
        block(2 * p, 2 * tq, False)
        return carry

    lax.fori_loop(0, qi // 2, pair_body, 0)

    @pl.when(qi % 2 == 1)
    def _():
        block(qi - 1, tq, False)

    block(qi, tq, True)

    lp = lam_ref[...]
    lam = (jnp.exp(jnp.sum(lp[0:1] * lp[1:2], axis=-1, keepdims=True))
           - jnp.exp(jnp.sum(lp[2:3] * lp[3:4], axis=-1, keepdims=True)) + _lambda_init(layer))
    gain = sg_ref[...] * (1.0 - _lambda_init(layer))
    for g in range(DIFF_GROUP):
        o1 = acc_ref[g * 2] * (1.0 / l_ref[g * 2:g * 2 + 1, :])
        o2 = acc_ref[g * 2 + 1] * (1.0 / l_ref[g * 2 + 1:g * 2 + 2, :])
        ot = o1 - lam * o2
        ot = ot * lax.rsqrt(jnp.mean(ot * ot, axis=0, keepdims=True) + LN_EPS)
        o_ref[:, g * dv:(g + 1) * dv] = (ot.T * gain).astype(o_ref.dtype)


def diff_prompt_attention(h, diff_lam, subln_g, slot, layer, n_batch, seq):
    tq = DIFF_TQ
    nq = seq // tq
    qw = DIFF_GROUP * 2 * DIFF_HEAD_DIM
    kw = 2 * DIFF_HEAD_DIM
    return pl.pallas_call(
        functools.partial(_diff_prompt_kernel, layer=layer),
        out_shape=jax.ShapeDtypeStruct((n_batch * seq, DIFF_Q_DIM), BF16),
        grid=(n_batch, DIFF_KV_HEADS, nq),
        in_specs=[
            pl.BlockSpec((None, 4, DIFF_HEAD_DIM), lambda b, h_, i: (slot, 0, 0)),
            pl.BlockSpec((None, 1, kw), lambda b, h_, i: (slot, 0, 0)),
            pl.BlockSpec((tq, qw), lambda b, h_, i: (b * nq + i, h_)),
            pl.BlockSpec((seq, kw), lambda b, h_, i: (b, DIFF_Q_DIM // kw + h_)),
            pl.BlockSpec((seq, kw), lambda b, h_, i: (b, (DIFF_Q_DIM + DIFF_K_DIM) // kw + h_)),
        ],
        out_specs=pl.BlockSpec((tq, qw), lambda b, h_, i: (b * nq + i, h_)),
        scratch_shapes=[
            pltpu.VMEM((2, 2 * DIFF_HEAD_DIM, DIFF_GROUP * tq), BF16),
            pltpu.VMEM((2 * DIFF_GROUP, tq), F32),
            pltpu.VMEM((2 * DIFF_GROUP, tq), F32),
            pltpu.VMEM((2 * DIFF_GROUP, 2 * DIFF_HEAD_DIM, tq), F32),
        ],
        compiler_params=_cparams(("arbitrary", "arbitrary", "arbitrary")),
        name="diff_prompt_attn",
    )(diff_lam, subln_g.reshape(-1, 1, kw), h, h, h)


def _diff_decode_kernel(pt_ref, lam_ref, sg_ref, slope_ref, wq_ref, kn_ref, vn_ref, *rest,
                        layer, n_pages, past_len):
    del pt_ref
    np_ = DECODE_PAGES
    k_refs = rest[:np_]
    v_refs = rest[np_:2 * np_]
    o_ref = rest[2 * np_]
    m_ref, l_ref, acc_ref = rest[2 * np_ + 1:]
    jc = pl.program_id(1)
    n_chunks = n_pages // np_
    dv = 2 * DIFF_HEAD_DIM

    @pl.when(jc == 0)
    def _():
        m_ref[...] = jnp.full(m_ref.shape, NEG_INF, F32)
        l_ref[...] = jnp.zeros(l_ref.shape, F32)
        acc_ref[...] = jnp.zeros(acc_ref.shape, F32)

    wq = wq_ref[...]
    slope = slope_ref[...]
    lane = lax.broadcasted_iota(jnp.int32, (1, PAGE_SIZE), 1)
    rowi = lax.broadcasted_iota(jnp.int32, (16, 1), 0)
    is_kv0 = (rowi % (2 * DIFF_GROUP)) < DIFF_GROUP

    s_pages = []
    for g in range(np_):
        kt = k_refs[g][...].astype(BF16)
        s = jnp.dot(wq, kt, preferred_element_type=F32)
        k_pos = (jc * np_ + g) * PAGE_SIZE + lane
        s_pages.append(s - slope * (past_len - k_pos).astype(F32))
    m_old = m_ref[...]
    m_new = m_old
    for s in s_pages:
        m_new = jnp.maximum(m_new, jnp.max(s, axis=-1, keepdims=True))
    alpha = jnp.exp(m_old - m_new)
    l_new = alpha * l_ref[...]
    acc = alpha * acc_ref[...]
    for g in range(np_):
        p = jnp.exp(s_pages[g] - m_new)
        l_new = l_new + jnp.sum(p, axis=-1, keepdims=True)
        pb = p.astype(BF16)
        v0 = v_refs[g][pl.ds(0, PAGE_SIZE, stride=DIFF_KV_HEADS), :].astype(BF16)
        v1 = v_refs[g][pl.ds(1, PAGE_SIZE, stride=DIFF_KV_HEADS), :].astype(BF16)
        acc = acc + jnp.where(is_kv0, jnp.dot(pb, v0, preferred_element_type=F32),
                              jnp.dot(pb, v1, preferred_element_type=F32))
    m_ref[...] = m_new
    l_ref[...] = l_new
    acc_ref[...] = acc

    @pl.when(jc == n_chunks - 1)
    def _():
        kn = kn_ref[...]
        vn = vn_ref[...]
        s_n = jnp.sum(wq.astype(F32) * kn, axis=-1, keepdims=True)
        m_o = m_ref[...]
        m_f = jnp.maximum(m_o, s_n)
        a_f = jnp.exp(m_o - m_f)
        p_n = jnp.exp(s_n - m_f)
        l_f = a_f * l_ref[...] + p_n
        v_rows = jnp.where(is_kv0, vn[:, :dv], vn[:, dv:])
        acc_f = a_f * acc_ref[...] + p_n * v_rows
        o_maps = acc_f / l_f
        lp = lam_ref[...]
        lam = (jnp.exp(jnp.sum(lp[0:1] * lp[1:2], axis=-1, keepdims=True))
               - jnp.exp(jnp.sum(lp[2:3] * lp[3:4], axis=-1, keepdims=True)) + _lambda_init(layer))
        o = o_maps[:DIFF_HEADS] - lam * o_maps[DIFF_HEADS:]
        gain = sg_ref[...] * (1.0 - _lambda_init(layer))
        o = o * lax.rsqrt(jnp.mean(o * o, axis=-1, keepdims=True) + LN_EPS) * gain
        o_ref[...] = o.astype(o_ref.dtype)


def diff_decode_attention(hs, kt_cache, v_cache, page_table, diff_lam, subln_g, slot, layer):
    n_seq = hs.shape[0]
    n_pages = page_table.shape[1]
    past_len = n_pages * PAGE_SIZE
    dh = DIFF_HEAD_DIM
    q = hs[:, :DIFF_Q_DIM].reshape(n_seq, DIFF_KV_HEADS, DIFF_GROUP, 2, dh) * (dh ** -0.5)
    eye = jnp.eye(2, dtype=F32)
    wq = jnp.einsum('bkgcd,kK,cC->bckgKCd', q, eye, eye).reshape(n_seq, 16, DIFF_K_DIM).astype(BF16)
    kn = hs[:, DIFF_Q_DIM:DIFF_Q_DIM + DIFF_K_DIM].reshape(n_seq, 1, DIFF_K_DIM)
    vn = hs[:, DIFF_Q_DIM + DIFF_K_DIM:].reshape(n_seq, 1, DIFF_V_DIM)
    heads = np.arange(DIFF_HEADS, dtype=np.float32).reshape(DIFF_KV_HEADS, DIFF_GROUP)
    slopes = np.tile((2.0 ** -(heads + 1.0)).reshape(1, DIFF_HEADS), (2, 1)).reshape(16, 1)
    slopes = jnp.asarray(slopes, F32)

    np_ = DECODE_PAGES
    n_chunks = n_pages // np_

    def page_map(g):
        return lambda b, jc, pt: (slot, pt[b, jc * np_ + g], 0, 0)

    in_specs = [
        pl.BlockSpec((None, 4, dh), lambda b, jc, pt: (slot, 0, 0)),
        pl.BlockSpec((None, 1, 2 * dh), lambda b, jc, pt: (slot, 0, 0)),
        pl.BlockSpec((16, 1), lambda b, jc, pt: (0, 0)),
        pl.BlockSpec((None, 16, DIFF_K_DIM), lambda b, jc, pt: (b, 0, 0)),
        pl.BlockSpec((None, 1, DIFF_K_DIM), lambda b, jc, pt: (b, 0, 0)),
        pl.BlockSpec((None, 1, DIFF_V_DIM), lambda b, jc, pt: (b, 0, 0)),
    ]
    in_specs += [pl.BlockSpec((None, None, DIFF_K_DIM, PAGE_SIZE), page_map(g)) for g in range(np_)]
    in_specs += [pl.BlockSpec((None, None, PAGE_SIZE * DIFF_KV_HEADS, 2 * dh), page_map(g))
                 for g in range(np_)]
    out = pl.pallas_call(
        functools.partial(_diff_decode_kernel, layer=layer, n_pages=n_pages, past_len=past_len),
        out_shape=jax.ShapeDtypeStruct((n_seq, DIFF_HEADS, 2 * dh), BF16),
        grid_spec=pltpu.PrefetchScalarGridSpec(
            num_scalar_prefetch=1,
            grid=(n_seq, n_chunks),
            in_specs=in_specs,
            out_specs=pl.BlockSpec((None, DIFF_HEADS, 2 * dh), lambda b, jc, pt: (b, 0, 0)),
            scratch_shapes=[
                pltpu.VMEM((16, 1), F32),
                pltpu.VMEM((16, 1), F32),
                pltpu.VMEM((16, 2 * dh), F32),
            ],
        ),
        compiler_params=_cparams(("arbitrary", "arbitrary")),
        name="diff_decode_attn",
    )(page_table, diff_lam, subln_g.reshape(-1, 1, 2 * dh), slopes, wq, kn, vn,
      *([kt_cache] * np_), *([v_cache] * np_))
    return out.reshape(n_seq, DIFF_Q_DIM)


def _swa_slopes():
    return 2.0 ** (-8.0 * np.arange(1, SWA_HEADS + 1, dtype=np.float32) / SWA_HEADS)


def _swa_prompt_kernel(sink_ref, slope_ref, q_ref, cur_ref, prev_ref, o_ref, *, n_blocks):
    w = WINDOW
    dh = SWA_HEAD_DIM
    bi = pl.program_id(0) % n_blocks
    krow = lax.broadcasted_iota(jnp.int32, (2 * w, w), 0)
    qcol = lax.broadcasted_iota(jnp.int32, (2 * w, w), 1)
    dist = (qcol - krow + w).astype(F32)
    in_seq = jnp.logical_or(krow >= w, (jnp.zeros_like(krow) + bi) > 0)
    ok = jnp.logical_and(jnp.logical_and(krow > qcol, krow <= qcol + w), in_seq)

    qt = (q_ref[...] * (dh ** -0.5)).T.astype(BF16)
    kk = jnp.concatenate([prev_ref[:, :SWA_KV_DIM], cur_ref[:, :SWA_KV_DIM]], axis=0).astype(BF16)
    vt = jnp.concatenate([prev_ref[:, SWA_KV_DIM:], cur_ref[:, SWA_KV_DIM:]], axis=0).T.astype(BF16)
    for kh in range(SWA_KV_HEADS):
        q_heads = jnp.concatenate(
            [qt[(kh * SWA_GROUP + g) * dh:(kh * SWA_GROUP + g + 1) * dh] for g in range(SWA_GROUP)],
            axis=1)
        s_all = jnp.dot(kk[:, kh * dh:(kh + 1) * dh], q_heads, preferred_element_type=F32)
        probs = []
        for g in range(SWA_GROUP):
            hd = kh * SWA_GROUP + g
            sink = sink_ref[hd]
            s = jnp.where(ok, s_all[:, g * w:(g + 1) * w] - slope_ref[hd] * dist, NEG_INF)
            m = jnp.maximum(jnp.max(s, axis=0, keepdims=True), sink)
            e = jnp.exp(s - m)
            den = jnp.sum(e, axis=0, keepdims=True) + jnp.exp(sink - m)
            probs.append((e * (1.0 / den)).astype(BF16))
        ot = jnp.dot(vt[kh * dh:(kh + 1) * dh], jnp.concatenate(probs, axis=1),
                     preferred_element_type=F32)
        for g in range(0, SWA_GROUP, 2):
            pair = jnp.concatenate([ot[:, g * w:(g + 1) * w], ot[:, (g + 1) * w:(g + 2) * w]], axis=0)
            hd = kh * SWA_GROUP + g
            o_ref[:, hd * dh:(hd + 2) * dh] = pair.T.astype(o_ref.dtype)


def swa_prompt_attention(h, sinks, n_batch, seq):
    nb = seq // WINDOW
    kvw = 2 * SWA_KV_DIM
    kv_col = SWA_Q_DIM // kvw
    slopes = jnp.asarray(_swa_slopes(), F32)
    smem = pl.BlockSpec(memory_space=pltpu.SMEM)
    return pl.pallas_call(
        functools.partial(_swa_prompt_kernel, n_blocks=nb),
        out_shape=jax.ShapeDtypeStruct((n_batch * seq, SWA_Q_DIM), BF16),
        grid=(n_batch * nb,),
        in_specs=[
            smem, smem,
            pl.BlockSpec((WINDOW, SWA_Q_DIM), lambda r: (r, 0)),
            pl.BlockSpec((WINDOW, kvw), lambda r: (r, kv_col)),
            pl.BlockSpec((WINDOW, kvw), lambda r: (jnp.maximum(r - 1, 0), kv_col)),
        ],
        out_specs=pl.BlockSpec((WINDOW, SWA_Q_DIM), lambda r: (r, 0)),
        compiler_params=_cparams(("arbitrary",)),
        name="swa_prompt_attn",
    )(sinks, slopes, h, h, h)


def _swa_decode_kernel(sink_ref, slope_ref, wq_ref, kn_ref, vn_ref, kt_ref, vt_ref, o_ref, *, win_buf):
    dh = SWA_HEAD_DIM
    lane = lax.broadcasted_iota(jnp.int32, (1, win_buf), 1)
    dist = (win_buf - lane).astype(F32)
    ok = dist < float(WINDOW)
    rowi = lax.broadcasted_iota(jnp.int32, (SWA_HEADS, 1), 0)
    is_kv0 = rowi < SWA_GROUP
    sink = sink_ref[...]
    slope = slope_ref[...]
    for sq in range(SWA_SEQS):
        wq = wq_ref[sq]
        kt = kt_ref[sq].astype(BF16)
        vt = vt_ref[sq].astype(BF16)
        s = jnp.dot(wq, kt, preferred_element_type=F32)
        s = jnp.where(ok, s - slope * dist, NEG_INF)
        s_n = jnp.sum(wq.astype(F32) * kn_ref[sq], axis=-1, keepdims=True)
        m = jnp.maximum(jnp.maximum(jnp.max(s, axis=-1, keepdims=True), s_n), sink)
        e = jnp.exp(s - m)
        e_n = jnp.exp(s_n - m)
        den = jnp.sum(e, axis=-1, keepdims=True) + e_n + jnp.exp(sink - m)
        inv = 1.0 / den
        pv = lax.dot_general((e * inv).astype(BF16), vt, (((1,), (1,)), ((), ())),
                             preferred_element_type=F32)
        pv = pv + (e_n * inv) * vn_ref[sq]
        o_ref[sq] = jnp.where(is_kv0, pv[:, :dh], pv[:, dh:]).astype(o_ref.dtype)


def swa_decode_attention(hs, kt_buf, vt_buf, sinks):
    n_seq = hs.shape[0]
    win_buf = kt_buf.shape[-1]
    dh = SWA_HEAD_DIM
    q = hs[:, :SWA_Q_DIM].reshape(n_seq, SWA_KV_HEADS, SWA_GROUP, dh) * (dh ** -0.5)
    eye = jnp.eye(SWA_KV_HEADS, dtype=F32)
    wq = jnp.einsum('bkgd,kK->bkgKd', q, eye).reshape(n_seq, SWA_HEADS, SWA_KV_DIM).astype(BF16)
    kn = hs[:, SWA_Q_DIM:SWA_Q_DIM + SWA_KV_DIM].reshape(n_seq, 1, SWA_KV_DIM)
    vn = hs[:, SWA_Q_DIM + SWA_KV_DIM:].reshape(n_seq, 1, SWA_KV_DIM)
    slopes = jnp.asarray(_swa_slopes().reshape(SWA_HEADS, 1), F32)
    sb = SWA_SEQS
    seq3 = lambda i: (i, 0, 0)
    out = pl.pallas_call(
        functools.partial(_swa_decode_kernel, win_buf=win_buf),
        out_shape=jax.ShapeDtypeStruct((n_seq, SWA_HEADS, dh), BF16),
        grid=(n_seq // sb,),
        in_specs=[
            pl.BlockSpec((SWA_HEADS, 1), lambda i: (0, 0)),
            pl.BlockSpec((SWA_HEADS, 1), lambda i: (0, 0)),
            pl.BlockSpec((sb, SWA_HEADS, SWA_KV_DIM), seq3),
            pl.BlockSpec((sb, 1, SWA_KV_DIM), seq3),
            pl.BlockSpec((sb, 1, SWA_KV_DIM), seq3),
            pl.BlockSpec((sb, SWA_KV_DIM, win_buf), seq3),
            pl.BlockSpec((sb, SWA_KV_DIM, win_buf), seq3),
        ],
        out_specs=pl.BlockSpec((sb, SWA_HEADS, dh), seq3),
        compiler_params=_cparams(("arbitrary",)),
        name="swa_decode_attn",
    )(sinks.reshape(SWA_HEADS, 1), slopes, wq, kn, vn, kt_buf, vt_buf)
    return out.reshape(n_seq, SWA_Q_DIM)


def _router_kernel(x_ref, w_ref, b_ref, e_ref, g_ref, r_ref, cnt_ref, carry_ref):
    tm = x_ref.shape[0]

    @pl.when(pl.program_id(0) == 0)
    def _():
        carry_ref[...] = jnp.zeros(carry_ref.shape, F32)

    logits = jnp.dot(x_ref[...].astype(BF16), w_ref[...].astype(BF16),
                     preferred_element_type=F32) + b_ref[...]
    lane = lax.broadcasted_iota(jnp.int32, (tm, N_EXPERTS), 1).astype(F32)
    work = logits
    vals, idxs = [], []
    for _ in range(TOP_K):
        mx = jnp.max(work, axis=-1, keepdims=True)
        ix = jnp.min(jnp.where(work == mx, lane, float(N_EXPERTS)), axis=-1, keepdims=True)
        vals.append(mx)
        idxs.append(ix)
        work = jnp.where(lane == ix, -jnp.inf, work)
    ex = [jnp.exp(v - vals[0]) for v in vals]
    den = ex[0] + ex[1] + ex[2] + ex[3]

    onehot = jnp.zeros((tm, N_EXPERTS), F32)
    for ix in idxs:
        onehot = onehot + (lane == ix).astype(F32)
    ri = lax.broadcasted_iota(jnp.int32, (tm, tm), 0)
    ci = lax.broadcasted_iota(jnp.int32, (tm, tm), 1)
    tri = (ci < ri).astype(BF16)
    before = jnp.dot(tri, onehot.astype(BF16), preferred_element_type=F32) + carry_ref[...]

    col4 = lax.broadcasted_iota(jnp.int32, (tm, TOP_K), 1)
    e_out = jnp.zeros((tm, TOP_K), jnp.int32)
    g_out = jnp.zeros((tm, TOP_K), F32)
    r_out = jnp.zeros((tm, TOP_K), jnp.int32)
    for k in range(TOP_K):
        rank = jnp.sum(jnp.where(lane == idxs[k], before, 0.0), axis=-1, keepdims=True)
        e_out = jnp.where(col4 == k, idxs[k].astype(jnp.int32), e_out)
        g_out = jnp.where(col4 == k, ex[k] / den, g_out)
        r_out = jnp.where(col4 == k, rank.astype(jnp.int32), r_out)
    e_ref[...] = e_out
    g_ref[...] = g_out
    r_ref[...] = r_out
    carry_ref[...] = carry_ref[...] + jnp.sum(onehot, axis=0, keepdims=True)
    cnt_ref[...] = carry_ref[...]


def router(x, w_router, b_router, layer):
    t, d = x.shape
    row = lambda i: (i, 0)
    return pl.pallas_call(
        _router_kernel,
        out_shape=(
            jax.ShapeDtypeStruct((t, TOP_K), jnp.int32),
            jax.ShapeDtypeStruct((t, TOP_K), F32),
            jax.ShapeDtypeStruct((t, TOP_K), jnp.int32),
            jax.ShapeDtypeStruct((1, N_EXPERTS), F32),
        ),
        grid=(t // ROW_TILE,),
        in_specs=[
            pl.BlockSpec((ROW_TILE, d), row),
            pl.BlockSpec((None, d, N_EXPERTS), lambda i: (layer, 0, 0)),
            pl.BlockSpec((None, 1, N_EXPERTS), lambda i: (layer, 0, 0)),
        ],
        out_specs=(
            pl.BlockSpec((ROW_TILE, TOP_K), row),
            pl.BlockSpec((ROW_TILE, TOP_K), row),
            pl.BlockSpec((ROW_TILE, TOP_K), row),
            pl.BlockSpec((1, N_EXPERTS), lambda i: (0, 0)),
        ),
        scratch_shapes=[pltpu.VMEM((1, N_EXPERTS), F32)],
        compiler_params=_cparams(("arbitrary",)),
        name="router",
    )(x, w_router, b_router.reshape(-1, 1, N_EXPERTS))


MAX_COPIES_PER_STEP = 2048


def _copies_per_step(n):
    return max(c for c in range(8, MAX_COPIES_PER_STEP + 1, 8) if n % c == 0)


def _token_copy(src_ref, dst_ref, sem, src_tok, dst_tok):
    return pltpu.make_async_copy(src_ref.at[src_tok], dst_ref.at[dst_tok], sem)


def _drain_token_copies(src_ref, dst_ref, sem, n):
    def drain(i, carry):
        _token_copy(src_ref, dst_ref, sem, 0, 0).wait()
        return carry

    lax.fori_loop(0, n, drain, 0, unroll=8)


def _scatter_tokens_kernel(idx_ref, src_ref, out_ref, sem, *, n):
    def issue(h, carry):
        for prio in range(2):
            i = 2 * h + prio
            tok = lax.shift_right_logical(i, TOP_K.bit_length() - 1)
            _token_copy(src_ref, out_ref, sem, tok, idx_ref[0, i]).start(priority=prio)
        return carry

    lax.fori_loop(0, n // 2, issue, 0, unroll=4)
    _drain_token_copies(src_ref, out_ref, sem, n)


def scatter_tokens(src_tiles, idx, n_out):
    t = src_tiles.shape[0] // TILE_ROWS
    n = _copies_per_step(idx.shape[0])
    steps = idx.shape[0] // n
    out = pl.pallas_call(
        functools.partial(_scatter_tokens_kernel, n=n),
        out_shape=jax.ShapeDtypeStruct((n_out, TILE_ROWS, 128), F32),
        grid=(steps,),
        in_specs=[pl.BlockSpec((None, 1, n), lambda s: (s, 0, 0), memory_space=pltpu.SMEM),
                  pl.BlockSpec((n // TOP_K, TILE_ROWS, 128), lambda s: (s, 0, 0))],
        out_specs=pl.BlockSpec(memory_space=pl.ANY),
        scratch_shapes=[pltpu.SemaphoreType.DMA(())],
        compiler_params=_cparams(("arbitrary",)),
        name="scatter_tokens",
    )(idx.reshape(steps, 1, n), src_tiles.reshape(t, TILE_ROWS, 128))
    return out.reshape(n_out * TILE_ROWS, 128)


def _experts_kernel(be_ref, nx_ref, bv_ref, x_ref, wgu_hbm, bgu_ref, wd_hbm, bd_ref, o_ref,
                    wgu_f32, wd_f32, wgu_bf, wd_bf, sems, *, layer):
    s = pl.program_id(0)
    e = be_ref[s]
    valid = bv_ref[s]
    new_expert = jnp.logical_or(s == 0, e != be_ref[jnp.maximum(s - 1, 0)])
    used = valid > 0

    def weight_copies(expert):
        return (pltpu.make_async_copy(wgu_hbm.at[layer, expert], wgu_f32, sems.at[0]),
                pltpu.make_async_copy(wd_hbm.at[layer, expert], wd_f32, sems.at[1]))

    @pl.when(s == 0)
    def _():
        for cp in weight_copies(e):
            cp.start()

    @pl.when(jnp.logical_and(new_expert, used))
    def _():
        for cp in weight_copies(e):
            cp.wait()
        wgu_bf[...] = wgu_f32[...].astype(BF16)
        wd_bf[...] = wd_f32[...].astype(BF16)

        @pl.when(nx_ref[s] >= 0)
        def _():
            for cp in weight_copies(nx_ref[s]):
                cp.start()

    @pl.when(used)
    def _():
        rows = lax.broadcasted_iota(jnp.int32, (MOE_TM, 1), 0)
        x = jnp.where(rows < valid, _load_token_tiles(x_ref), 0.0).astype(BF16)
        h = jnp.dot(x, wgu_bf[...], preferred_element_type=F32) + bgu_ref[...]
        gate = jnp.minimum(h[:, :D_FF], SWIGLU_LIMIT)
        up = jnp.clip(h[:, D_FF:], -SWIGLU_LIMIT, SWIGLU_LIMIT)
        act = (up + 1.0) * (gate * jax.nn.sigmoid(SWIGLU_ALPHA * gate))
        y = jnp.dot(act.astype(BF16), wd_bf[...], preferred_element_type=F32) + bd_ref[...]
        _store_token_tiles(o_ref, y)

    @pl.when(jnp.logical_not(used))
    def _():
        o_ref[...] = jnp.zeros(o_ref.shape, o_ref.dtype)


def experts(xs, block_e, next_e, block_valid, w_gu, b_gu, w_down, b_down, layer):
    d = D_MODEL
    n_blocks = xs.shape[0] // (MOE_TM * TILE_ROWS)
    tile_block = pl.BlockSpec((MOE_TM * TILE_ROWS, 128), lambda s, be, nx, bv: (s, 0))
    any_spec = pl.BlockSpec(memory_space=pl.ANY)
    return pl.pallas_call(
        functools.partial(_experts_kernel, layer=layer),
        out_shape=jax.ShapeDtypeStruct(xs.shape, F32),
        grid_spec=pltpu.PrefetchScalarGridSpec(
            num_scalar_prefetch=3,
            grid=(n_blocks,),
            in_specs=[
                tile_block,
                any_spec,
                pl.BlockSpec((None, None, 1, 2 * D_FF), lambda s, be, nx, bv: (layer, be[s], 0, 0)),
                any_spec,
                pl.BlockSpec((None, None, 1, d), lambda s, be, nx, bv: (layer, be[s], 0, 0)),
            ],
            out_specs=tile_block,
            scratch_shapes=[
                pltpu.VMEM((d, 2 * D_FF), F32),
                pltpu.VMEM((D_FF, d), F32),
                pltpu.VMEM((d, 2 * D_FF), BF16),
                pltpu.VMEM((D_FF, d), BF16),
                pltpu.SemaphoreType.DMA((2,)),
            ],
        ),
        compiler_params=_cparams(("arbitrary",)),
        name="experts",
    )(block_e, next_e, block_valid, xs, w_gu, b_gu.reshape(DEPTH, N_EXPERTS, 1, 2 * D_FF),
      w_down, b_down.reshape(DEPTH, N_EXPERTS, 1, d))


def _combine_ln_kernel(idx0_ref, idxn_ref, ys_ref, x_ref, gt_ref, g_ref, b_ref, o_ref, buf_ref, sems):
    s = pl.program_id(0)
    n_tiles = pl.num_programs(0)
    n = TOP_K * ROW_TILE

    def copy(slot, i, src_tok):
        dst = buf_ref.at[slot, pl.ds(pl.multiple_of(i * TILE_ROWS, TILE_ROWS), TILE_ROWS), :]
        return pltpu.make_async_copy(ys_ref.at[src_tok], dst, sems.at[slot])

    def fetch(slot, idx_ref):
        def issue(h, carry):
            for prio in range(2):
                i = 2 * h + prio
                copy(slot, i, idx_ref[0, i]).start(priority=prio)
            return carry

        lax.fori_loop(0, n // 2, issue, 0, unroll=4)

    def drain(slot):
        def wait(i, carry):
            copy(slot, 0, 0).wait()
            return carry

        lax.fori_loop(0, n, wait, 0, unroll=8)

    @pl.when(s == 0)
    def _():
        fetch(0, idx0_ref)

    for slot in range(2):
        @pl.when(s % 2 == slot)
        def _():
            @pl.when(s + 1 < n_tiles)
            def _():
                fetch(1 - slot, idxn_ref)

            drain(slot)
            gt = gt_ref[...]
            f = None
            for k in range(TOP_K):
                base = k * ROW_TILE * TILE_ROWS
                y = jnp.concatenate(
                    [buf_ref[slot, pl.ds(base + c, ROW_TILE, stride=TILE_ROWS), :] for c in range(TILE_ROWS)],
                    axis=-1) * gt[:, k:k + 1]
                f = y if f is None else f + y
            o_ref[...] = _layer_norm_rows(ALPHA * x_ref[...] + f, g_ref[...], b_ref[...])


def combine_ln(x, ys, dest, gates, g_all, b_all, layer):
    t, d = x.shape
    nt = t // ROW_TILE
    n = TOP_K * ROW_TILE
    row = lambda i: (i, 0)
    idx = dest.reshape(nt, ROW_TILE, TOP_K).transpose(0, 2, 1).reshape(nt, 1, n)
    return pl.pallas_call(
        _combine_ln_kernel,
        out_shape=jax.ShapeDtypeStruct((t, d), F32),
        grid=(nt,),
        in_specs=[
            pl.BlockSpec((None, 1, n), lambda i: (0, 0, 0), memory_space=pltpu.SMEM),
            pl.BlockSpec((None, 1, n), lambda i: (jnp.minimum(i + 1, nt - 1), 0, 0),
                         memory_space=pltpu.SMEM),
            pl.BlockSpec(memory_space=pl.ANY),
            pl.BlockSpec((ROW_TILE, d), row),
            pl.BlockSpec((ROW_TILE, TOP_K), row),
            pl.BlockSpec((None, 1, d), lambda i: (layer, 0, 0)),
            pl.BlockSpec((None, 1, d), lambda i: (layer, 0, 0)),
        ],
        out_specs=pl.BlockSpec((ROW_TILE, d), row),
        scratch_shapes=[pltpu.VMEM((2, n * TILE_ROWS, 128), F32),
                        pltpu.SemaphoreType.DMA((2,))],
        compiler_params=_cparams(("arbitrary",)),
        name="combine_ln",
    )(idx, idx, ys.reshape(-1, TILE_ROWS, 128), x, gates, g_all.reshape(-1, 1, d), b_all.reshape(-1, 1, d))


def moe_layer(x, x_tiles, layer, w_router, b_router, w_gu, b_gu, w_down, b_down, ln_g, ln_b):
    t, d = x.shape
    top_e, gates, rank, counts = router(x, w_router, b_router, layer)
    counts = counts.reshape(N_EXPERTS).astype(jnp.int32)
    padded = (counts + MOE_TM - 1) // MOE_TM * MOE_TM
    pad_end = jnp.cumsum(padded)
    pad_start = pad_end - padded
    n_blocks = -(-t * TOP_K // MOE_TM) + N_EXPERTS
    dest = pad_start[top_e] + rank
    block_start = jnp.arange(n_blocks, dtype=jnp.int32) * MOE_TM
    block_e = jnp.minimum(jnp.sum(pad_end[None, :] <= block_start[:, None], axis=1),
                          N_EXPERTS - 1).astype(jnp.int32)
    n_used = pad_end[-1] // MOE_TM
    block_valid = jnp.where(block_start < pad_end[-1],
                            jnp.clip(pad_start[block_e] + counts[block_e] - block_start, 0, MOE_TM),
                            0).astype(jnp.int32)
    after = pad_end[block_e] // MOE_TM
    next_e = jnp.where(after < n_used, block_e[jnp.minimum(after, n_blocks - 1)], -1).astype(jnp.int32)
    xs = scatter_tokens(x_tiles, dest.reshape(-1), n_blocks * MOE_TM)
    ys = experts(xs, block_e, next_e, block_valid, w_gu, b_gu, w_down, b_down, layer)
    return combine_ln(x, ys, dest, gates, ln_g, ln_b, layer)


def kernel(x_prompt, x_sample, cache_k_diff, cache_v_diff, cache_k_swa, cache_v_swa, page_table,
           ln1_g, ln1_b, ln2_g, ln2_b, w_qkv_diff, diff_lam, diff_subln_g, w_o_diff,
           w_qkv_swa, b_qkv_swa, swa_sinks, w_o_swa, b_o_swa,
           w_router, b_router, w_gu, b_gu, w_down, b_down):
    n_batch, seq, d = x_prompt.shape
    n_seq = x_sample.shape[0]
    tp = n_batch * seq
    x = jnp.concatenate([x_prompt.reshape(tp, d), x_sample.reshape(n_seq, d)], axis=0)

    n_layers_diff, n_phys = cache_k_diff.shape[:2]
    kt_cache = jnp.transpose(cache_k_diff, (0, 1, 3, 4, 5, 2)).reshape(
        n_layers_diff, n_phys, DIFF_K_DIM, PAGE_SIZE)
    v_cache = cache_v_diff.reshape(n_layers_diff, n_phys, PAGE_SIZE * DIFF_KV_HEADS, 2 * DIFF_HEAD_DIM)
    win_buf = cache_k_swa.shape[2]
    kt_swa = jnp.transpose(cache_k_swa, (0, 1, 3, 4, 2)).reshape(-1, n_seq, SWA_KV_DIM, win_buf)
    vt_swa = jnp.transpose(cache_v_swa, (0, 1, 3, 4, 2)).reshape(-1, n_seq, SWA_KV_DIM, win_buf)

    zero_bias_qkv = jnp.zeros((1, w_qkv_diff.shape[-1]), F32)
    zero_bias_o = jnp.zeros((1, d), F32)

    kdp, vdp, kds, vds = [], [], [], []
    ksp, vsp, kss, vss = [], [], [], []
    for i in range(DEPTH):
        j = i // 2
        if i % 2 == 0:
            h = linear(x, w_qkv_diff, j, zero_bias_qkv, 512)
            hs = h[tp:]
            a_p = diff_prompt_attention(h, diff_lam, diff_subln_g, j, i, n_batch, seq)
            a_s = diff_decode_attention(hs, kt_cache, v_cache, page_table, diff_lam, diff_subln_g, j, i)
            kq, vq = DIFF_Q_DIM, DIFF_Q_DIM + DIFF_K_DIM
            kdp.append(h[:tp, kq:vq].reshape(n_batch, seq, DIFF_KV_HEADS, 2, DIFF_HEAD_DIM))
            vdp.append(h[:tp, vq:].reshape(n_batch, seq, DIFF_KV_HEADS, 2 * DIFF_HEAD_DIM))
            kds.append(hs[:, kq:vq].reshape(n_seq, 1, DIFF_KV_HEADS, 2, DIFF_HEAD_DIM))
            vds.append(hs[:, vq:].reshape(n_seq, 1, DIFF_KV_HEADS, 2 * DIFF_HEAD_DIM))
            a = jnp.concatenate([a_p, a_s], axis=0)
            x, x_tiles = oproj_ln(a, w_o_diff, j, zero_bias_o, x, ln1_g, ln1_b, i)
        else:
            h = linear(x, w_qkv_swa, j, b_qkv_swa[j].reshape(1, -1), 640)
            hs = h[tp:]
            a_p = swa_prompt_attention(h, swa_sinks[j], n_batch, seq)
            a_s = swa_decode_attention(hs, kt_swa[j], vt_swa[j], swa_sinks[j])
            kq, vq = SWA_Q_DIM, SWA_Q_DIM + SWA_KV_DIM
            keep = min(WINDOW, seq)
            hp = h[:tp].reshape(n_batch, seq, -1)[:, seq - keep:]
            ksp.append(hp[..., kq:vq].reshape(n_batch, keep, SWA_KV_HEADS, SWA_HEAD_DIM))
            vsp.append(hp[..., vq:].reshape(n_batch, keep, SWA_KV_HEADS, SWA_HEAD_DIM))
            k_new = hs[:, kq:vq].reshape(n_seq, 1, SWA_KV_HEADS, SWA_HEAD_DIM)
            v_new = hs[:, vq:].reshape(n_seq, 1, SWA_KV_HEADS, SWA_HEAD_DIM)
            kss.append(jnp.concatenate([cache_k_swa[j], k_new], axis=1)[:, -win_buf:])
            vss.append(jnp.concatenate([cache_v_swa[j], v_new], axis=1)[:, -win_buf:])
            a = jnp.concatenate([a_p, a_s], axis=0)
            x, x_tiles = oproj_ln(a, w_o_swa, j, b_o_swa[j].reshape(1, -1), x, ln1_g, ln1_b, i)
        x = moe_layer(x, x_tiles, i, w_router, b_router, w_gu, b_gu, w_down, b_down, ln2_g, ln2_b)

    return (x[:tp].reshape(n_batch, seq, d), x[tp:].reshape(n_seq, 1, d),
            jnp.stack(kdp), jnp.stack(vdp), jnp.stack(kds), jnp.stack(vds),
            jnp.stack(ksp), jnp.stack(vsp), jnp.stack(kss), jnp.stack(vss))
```

```python
import functools
import math

import jax
import jax.numpy as jnp
import numpy as np
from jax import lax
from jax.experimental import pallas as pl
from jax.experimental.pallas import tpu as pltpu

F32 = jnp.float32
BF16 = jnp.bfloat16

D_MODEL = 1024
DEPTH = 4
PAGE_SIZE = 128
DIFF_HEAD_DIM = 64
DIFF_HEADS = 8
DIFF_KV_HEADS = 2
DIFF_GROUP = 4
DIFF_Q_DIM = 1024
DIFF_K_DIM = 256
DIFF_V_DIM = 256
SWA_HEAD_DIM = 64
SWA_HEADS = 16
SWA_KV_HEADS = 2
SWA_GROUP = 8
SWA_Q_DIM = 1024
SWA_KV_DIM = 128
WINDOW = 128
N_EXPERTS = 32
TOP_K = 4
D_FF = 1024
SWIGLU_ALPHA = 1.702
SWIGLU_LIMIT = 7.0
ALPHA = (2 * DEPTH) ** 0.25
LN_EPS = 1e-5
NEG_INF = -1e30

ROW_TILE = 384
DIFF_TQ = 256
MOE_TM = 512
DECODE_PAGES = 16
SWA_SEQS = 8
VMEM_LIMIT = 56 * 1024 * 1024


def _lambda_init(layer):
    return 0.8 - 0.6 * math.exp(-0.3 * layer)


def _cparams(sem):
    return pltpu.CompilerParams(dimension_semantics=sem, vmem_limit_bytes=VMEM_LIMIT)


def _linear_kernel(x_ref, w_ref, b_ref, o_ref, *rest, col_ranges):
    col_refs, wbf_ref = rest[:-1], rest[-1]

    @pl.when(pl.program_id(0) == 0)
    def _():
        wbf_ref[...] = w_ref[...].astype(BF16)

    acc = jnp.dot(x_ref[...].astype(BF16), wbf_ref[...], preferred_element_type=F32) + b_ref[...]
    o_ref[...] = acc
    for (lo, hi), ref in zip(col_ranges, col_refs):
        ref[...] = acc[:, lo:hi]


def linear(x, w_all, layer, bias, col_ranges=()):
    t, k = x.shape
    n = w_all.shape[-1]
    row = lambda i: (i, 0)
    outs = pl.pallas_call(
        functools.partial(_linear_kernel, col_ranges=tuple(col_ranges)),
        out_shape=(jax.ShapeDtypeStruct((t, n), F32),
                   *[jax.ShapeDtypeStruct((t, hi - lo), F32) for lo, hi in col_ranges]),
        grid=(t // ROW_TILE,),
        in_specs=[
            pl.BlockSpec((ROW_TILE, k), row),
            pl.BlockSpec((None, k, n), lambda i: (layer, 0, 0)),
            pl.BlockSpec((1, n), lambda i: (0, 0)),
        ],
        out_specs=(pl.BlockSpec((ROW_TILE, n), row),
                   *[pl.BlockSpec((ROW_TILE, hi - lo), row) for lo, hi in col_ranges]),
        scratch_shapes=[pltpu.VMEM((k, n), BF16)],
        compiler_params=_cparams(("arbitrary",)),
        name="linear",
    )(x, w_all, bias)
    return outs[0], outs[1:]


def _layer_norm_rows(z, g, b):
    mu = jnp.mean(z, axis=-1, keepdims=True)
    zc = z - mu
    var = jnp.mean(zc * zc, axis=-1, keepdims=True)
    return zc * lax.rsqrt(var + LN_EPS) * g + b


TILE_ROWS = D_MODEL // 128


def _store_token_tiles(ref, val):
    n = val.shape[0]
    for c in range(TILE_ROWS):
        ref[pl.ds(c, n, stride=TILE_ROWS), :] = val[:, c * 128:(c + 1) * 128]


def _load_token_tiles(ref):
    n = ref.shape[0] // TILE_ROWS
    return jnp.concatenate([ref[pl.ds(c, n, stride=TILE_ROWS), :] for c in range(TILE_ROWS)], axis=-1)


def _oproj_ln_kernel(a_ref, w_ref, bo_ref, x_ref, g_ref, b_ref, wr_ref, br_ref,
                     o_ref, ot_ref, e_ref, gt_ref, r_ref, cnt_ref, wbf_ref, carry_ref):
    @pl.when(pl.program_id(0) == 0)
    def _():
        wbf_ref[...] = w_ref[...].astype(BF16)

    y = jnp.dot(a_ref[...], wbf_ref[...], preferred_element_type=F32) + bo_ref[...]
    z = _layer_norm_rows(ALPHA * x_ref[...] + y, g_ref[...], b_ref[...])
    o_ref[...] = z
    _store_token_tiles(ot_ref, z)
    _route_rows(z, wr_ref, br_ref, e_ref, gt_ref, r_ref, cnt_ref, carry_ref)


def oproj_ln_route(a, w_all, layer_slot, b_o, x, g_all, b_all, w_router, b_router, layer):
    t, d = x.shape
    row = lambda i: (i, 0)
    return pl.pallas_call(
        _oproj_ln_kernel,
        out_shape=(jax.ShapeDtypeStruct((t, d), F32),
                   jax.ShapeDtypeStruct((t * TILE_ROWS, 128), F32),
                   jax.ShapeDtypeStruct((t, TOP_K), jnp.int32),
                   jax.ShapeDtypeStruct((t, TOP_K), F32),
                   jax.ShapeDtypeStruct((t, TOP_K), jnp.int32),
                   jax.ShapeDtypeStruct((1, N_EXPERTS), F32)),
        grid=(t // ROW_TILE,),
        in_specs=[
            pl.BlockSpec((ROW_TILE, d), row),
            pl.BlockSpec((None, d, d), lambda i: (layer_slot, 0, 0)),
            pl.BlockSpec((1, d), lambda i: (0, 0)),
            pl.BlockSpec((ROW_TILE, d), row),
            pl.BlockSpec((None, 1, d), lambda i: (layer, 0, 0)),
            pl.BlockSpec((None, 1, d), lambda i: (layer, 0, 0)),
            pl.BlockSpec((None, d, N_EXPERTS), lambda i: (layer, 0, 0)),
            pl.BlockSpec((None, 1, N_EXPERTS), lambda i: (layer, 0, 0)),
        ],
        out_specs=(pl.BlockSpec((ROW_TILE, d), row),
                   pl.BlockSpec((ROW_TILE * TILE_ROWS, 128), row),
                   pl.BlockSpec((ROW_TILE, TOP_K), row),
                   pl.BlockSpec((ROW_TILE, TOP_K), row),
                   pl.BlockSpec((ROW_TILE, TOP_K), row),
                   pl.BlockSpec((1, N_EXPERTS), lambda i: (0, 0))),
        scratch_shapes=[pltpu.VMEM((d, d), BF16), pltpu.VMEM((1, N_EXPERTS), F32)],
        compiler_params=_cparams(("arbitrary",)),
        name="oproj_ln_route",
    )(a, w_all, b_o, x, g_all.reshape(-1, 1, d), b_all.reshape(-1, 1, d),
      w_router, b_router.reshape(-1, 1, N_EXPERTS))


def _diff_prompt_kernel(lam_ref, sg_ref, q_ref, k_ref, v_ref, o_ref,
                        qt_ref, m_ref, l_ref, acc_ref, *, layer):
    tq = DIFF_TQ
    dh = DIFF_HEAD_DIM
    dv = 2 * dh
    kvh = pl.program_id(1)
    qi = pl.program_id(2)

    qt = (q_ref[...] * (dh ** -0.5)).T
    arow = lax.broadcasted_iota(jnp.int32, (dh, tq), 0)
    q_off = lax.broadcasted_iota(jnp.int32, (dh, tq), 1).astype(F32)
    for g in range(DIFF_GROUP):
        slope = jnp.where(kvh == 0, 2.0 ** -(g + 1), 2.0 ** -(DIFF_GROUP + g + 1)).astype(F32)
        extra = jnp.where(arow < 2, slope, jnp.where(arow == 2, -slope * q_off, 0.0))
        q0 = qt[(g * 2) * dh:(g * 2 + 1) * dh]
        q1 = qt[(g * 2 + 1) * dh:(g * 2 + 2) * dh]
        qt_ref[0, :, g * tq:(g + 1) * tq] = jnp.concatenate([q0, extra], axis=0).astype(BF16)
        qt_ref[1, :, g * tq:(g + 1) * tq] = jnp.concatenate([extra, q1], axis=0).astype(BF16)
    m_ref[...] = jnp.full(m_ref.shape, NEG_INF, F32)
    l_ref[...] = jnp.zeros(l_ref.shape, F32)
    acc_ref[...] = jnp.zeros(acc_ref.shape, F32)

    krow = lax.broadcasted_iota(jnp.int32, (tq, tq), 0)
    qcol = lax.broadcasted_iota(jnp.int32, (tq, tq), 1)
    causal = krow <= qcol

    def block(j, nk, masked):
        start = pl.multiple_of(j * tq, tq)
        kb = k_ref[pl.ds(start, nk), :]
        vt = v_ref[pl.ds(start, nk), :].T.astype(BF16)
        lane = lax.broadcasted_iota(jnp.int32, (nk, dv), 1)
        row = lax.broadcasted_iota(jnp.int32, (nk, dv), 0)
        k_off = jnp.bitwise_and(row, tq - 1).astype(F32)
        shift = ((j - qi) * tq).astype(F32) + jnp.bitwise_and(row, -tq).astype(F32)
        keys = [
            jnp.where(lane < dh, kb,
                      jnp.where(lane == dh, k_off,
                                jnp.where(lane == dh + 1, shift, jnp.where(lane == dh + 2, 1.0, 0.0)))),
            jnp.where(lane >= dh, kb,
                      jnp.where(lane == 0, k_off,
                                jnp.where(lane == 1, shift, jnp.where(lane == 2, 1.0, 0.0)))),
        ]
        for c in range(2):
            s_all = jnp.dot(keys[c].astype(BF16), qt_ref[c], preferred_element_type=F32)
            for g in range(DIFF_GROUP):
                idx = g * 2 + c
                s = s_all[:, g * tq:(g + 1) * tq]
                if masked:
                    s = jnp.where(causal, s, NEG_INF)
                m_old = m_ref[idx:idx + 1, :]
                m_new = jnp.maximum(m_old, jnp.max(s, axis=0, keepdims=True))
                alpha = jnp.exp(m_old - m_new)
                p = jnp.exp(s - m_new)
                l_ref[idx:idx + 1, :] = alpha * l_ref[idx:idx + 1, :] + jnp.sum(p, axis=0, keepdims=True)
                acc_ref[idx] = alpha * acc_ref[idx] + jnp.dot(vt, p.astype(BF16),
                                                              preferred_element_type=F32)
                m_ref[idx:idx + 1, :] = m_new

    def pair_body(p, carry):
        block(2 * p, 2 * tq, False)
        return carry

    lax.fori_loop(0, qi // 2, pair_body, 0)

    @pl.when(qi % 2 == 1)
    def _():
        block(qi - 1, tq, False)

    block(qi, tq, True)

    lp = lam_ref[...]
    lam = (jnp.exp(jnp.sum(lp[0:1] * lp[1:2], axis=-1, keepdims=True))
           - jnp.exp(jnp.sum(lp[2:3] * lp[3:4], axis=-1, keepdims=True)) + _lambda_init(layer))
    gain = sg_ref[...] * (1.0 - _lambda_init(layer))
    for g in range(DIFF_GROUP):
        o1 = acc_ref[g * 2] * (1.0 / l_ref[g * 2:g * 2 + 1, :])
        o2 = acc_ref[g * 2 + 1] * (1.0 / l_ref[g * 2 + 1:g * 2 + 2, :])
        ot = o1 - lam * o2
        ot = ot * lax.rsqrt(jnp.mean(ot * ot, axis=0, keepdims=True) + LN_EPS)
        o_ref[:, g * dv:(g + 1) * dv] = (ot.T * gain).astype(o_ref.dtype)


def diff_prompt_attention(h, diff_lam, subln_g, slot, layer, n_batch, seq):
    tq = DIFF_TQ
    nq = seq // tq
    qw = DIFF_GROUP * 2 * DIFF_HEAD_DIM
    kw = 2 * DIFF_HEAD_DIM
    return pl.pallas_call(
        functools.partial(_diff_prompt_kernel, layer=layer),
        out_shape=jax.ShapeDtypeStruct((n_batch * seq, DIFF_Q_DIM), BF16),
        grid=(n_batch, DIFF_KV_HEADS, nq),
        in_specs=[
            pl.BlockSpec((None, 4, DIFF_HEAD_DIM), lambda b, h_, i: (slot, 0, 0)),
            pl.BlockSpec((None, 1, kw), lambda b, h_, i: (slot, 0, 0)),
            pl.BlockSpec((tq, qw), lambda b, h_, i: (b * nq + i, h_)),
            pl.BlockSpec((seq, kw), lambda b, h_, i: (b, DIFF_Q_DIM // kw + h_)),
            pl.BlockSpec((seq, kw), lambda b, h_, i: (b, (DIFF_Q_DIM + DIFF_K_DIM) // kw + h_)),
        ],
        out_specs=pl.BlockSpec((tq, qw), lambda b, h_, i: (b * nq + i, h_)),
        scratch_shapes=[
            pltpu.VMEM((2, 2 * DIFF_HEAD_DIM, DIFF_GROUP * tq), BF16),
            pltpu.VMEM((2 * DIFF_GROUP, tq), F32),
            pltpu.VMEM((2 * DIFF_GROUP, tq), F32),
            pltpu.VMEM((2 * DIFF_GROUP, 2 * DIFF_HEAD_DIM, tq), F32),
        ],
        compiler_params=_cparams(("arbitrary", "arbitrary", "arbitrary")),
        name="diff_prompt_attn",
    )(diff_lam, subln_g.reshape(-1, 1, kw), h, h, h)


def _diff_decode_kernel(pt_ref, lam_ref, sg_ref, slope_ref, wq_ref, kn_ref, vn_ref, *rest,
                        layer, n_pages, past_len):
    del pt_ref
    np_ = DECODE_PAGES
    k_refs = rest[:np_]
    v_refs = rest[np_:2 * np_]
    o_ref = rest[2 * np_]
    m_ref, l_ref, acc_ref = rest[2 * np_ + 1:]
    jc = pl.program_id(1)
    n_chunks = n_pages // np_
    dv = 2 * DIFF_HEAD_DIM

    @pl.when(jc == 0)
    def _():
        m_ref[...] = jnp.full(m_ref.shape, NEG_INF, F32)
        l_ref[...] = jnp.zeros(l_ref.shape, F32)
        acc_ref[...] = jnp.zeros(acc_ref.shape, F32)

    wq = wq_ref[...]
    slope = slope_ref[...]
    lane = lax.broadcasted_iota(jnp.int32, (1, PAGE_SIZE), 1)
    rowi = lax.broadcasted_iota(jnp.int32, (16, 1), 0)
    is_kv0 = (rowi % (2 * DIFF_GROUP)) < DIFF_GROUP

    s_pages = []
    for g in range(np_):
        kt = k_refs[g][...].astype(BF16)
        s = jnp.dot(wq, kt, preferred_element_type=F32)
        k_pos = (jc * np_ + g) * PAGE_SIZE + lane
        s_pages.append(s - slope * (past_len - k_pos).astype(F32))
    m_old = m_ref[...]
    m_new = m_old
    for s in s_pages:
        m_new = jnp.maximum(m_new, jnp.max(s, axis=-1, keepdims=True))
    alpha = jnp.exp(m_old - m_new)
    l_new = alpha * l_ref[...]
    acc = alpha * acc_ref[...]
    for g in range(np_):
        p = jnp.exp(s_pages[g] - m_new)
        l_new = l_new + jnp.sum(p, axis=-1, keepdims=True)
        pb = p.astype(BF16)
        v0 = v_refs[g][pl.ds(0, PAGE_SIZE, stride=DIFF_KV_HEADS), :].astype(BF16)
        v1 = v_refs[g][pl.ds(1, PAGE_SIZE, stride=DIFF_KV_HEADS), :].astype(BF16)
        acc = acc + jnp.where(is_kv0, jnp.dot(pb, v0, preferred_element_type=F32),
                              jnp.dot(pb, v1, preferred_element_type=F32))
    m_ref[...] = m_new
    l_ref[...] = l_new
    acc_ref[...] = acc

    @pl.when(jc == n_chunks - 1)
    def _():
        kn = kn_ref[...]
        vn = vn_ref[...]
        s_n = jnp.sum(wq.astype(F32) * kn, axis=-1, keepdims=True)
        m_o = m_ref[...]
        m_f = jnp.maximum(m_o, s_n)
        a_f = jnp.exp(m_o - m_f)
        p_n = jnp.exp(s_n - m_f)
        l_f = a_f * l_ref[...] + p_n
        v_rows = jnp.where(is_kv0, vn[:, :dv], vn[:, dv:])
        acc_f = a_f * acc_ref[...] + p_n * v_rows
        o_maps = acc_f / l_f
        lp = lam_ref[...]
        lam = (jnp.exp(jnp.sum(lp[0:1] * lp[1:2], axis=-1, keepdims=True))
               - jnp.exp(jnp.sum(lp[2:3] * lp[3:4], axis=-1, keepdims=True)) + _lambda_init(layer))
        o = o_maps[:DIFF_HEADS] - lam * o_maps[DIFF_HEADS:]
        gain = sg_ref[...] * (1.0 - _lambda_init(layer))
        o = o * lax.rsqrt(jnp.mean(o * o, axis=-1, keepdims=True) + LN_EPS) * gain
        o_ref[...] = o.astype(o_ref.dtype)


def diff_decode_attention(hs, kt_cache, v_cache, page_table, diff_lam, subln_g, slot, layer):
    n_seq = hs.shape[0]
    n_pages = page_table.shape[1]
    past_len = n_pages * PAGE_SIZE
    dh = DIFF_HEAD_DIM
    q = hs[:, :DIFF_Q_DIM].reshape(n_seq, DIFF_KV_HEADS, DIFF_GROUP, 2, dh) * (dh ** -0.5)
    eye = jnp.eye(2, dtype=F32)
    wq = jnp.einsum('bkgcd,kK,cC->bckgKCd', q, eye, eye).reshape(n_seq, 16, DIFF_K_DIM).astype(BF16)
    kn = hs[:, DIFF_Q_DIM:DIFF_Q_DIM + DIFF_K_DIM].reshape(n_seq, 1, DIFF_K_DIM)
    vn = hs[:, DIFF_Q_DIM + DIFF_K_DIM:].reshape(n_seq, 1, DIFF_V_DIM)
    heads = np.arange(DIFF_HEADS, dtype=np.float32).reshape(DIFF_KV_HEADS, DIFF_GROUP)
    slopes = np.tile((2.0 ** -(heads + 1.0)).reshape(1, DIFF_HEADS), (2, 1)).reshape(16, 1)
    slopes = jnp.asarray(slopes, F32)

    np_ = DECODE_PAGES
    n_chunks = n_pages // np_

    def page_map(g):
        return lambda b, jc, pt: (slot, pt[b, jc * np_ + g], 0, 0)

    in_specs = [
        pl.BlockSpec((None, 4, dh), lambda b, jc, pt: (slot, 0, 0)),
        pl.BlockSpec((None, 1, 2 * dh), lambda b, jc, pt: (slot, 0, 0)),
        pl.BlockSpec((16, 1), lambda b, jc, pt: (0, 0)),
        pl.BlockSpec((None, 16, DIFF_K_DIM), lambda b, jc, pt: (b, 0, 0)),
        pl.BlockSpec((None, 1, DIFF_K_DIM), lambda b, jc, pt: (b, 0, 0)),
        pl.BlockSpec((None, 1, DIFF_V_DIM), lambda b, jc, pt: (b, 0, 0)),
    ]
    in_specs += [pl.BlockSpec((None, None, DIFF_K_DIM, PAGE_SIZE), page_map(g)) for g in range(np_)]
    in_specs += [pl.BlockSpec((None, None, PAGE_SIZE * DIFF_KV_HEADS, 2 * dh), page_map(g))
                 for g in range(np_)]
    out = pl.pallas_call(
        functools.partial(_diff_decode_kernel, layer=layer, n_pages=n_pages, past_len=past_len),
        out_shape=jax.ShapeDtypeStruct((n_seq, DIFF_HEADS, 2 * dh), BF16),
        grid_spec=pltpu.PrefetchScalarGridSpec(
            num_scalar_prefetch=1,
            grid=(n_seq, n_chunks),
            in_specs=in_specs,
            out_specs=pl.BlockSpec((None, DIFF_HEADS, 2 * dh), lambda b, jc, pt: (b, 0, 0)),
            scratch_shapes=[
                pltpu.VMEM((16, 1), F32),
                pltpu.VMEM((16, 1), F32),
                pltpu.VMEM((16, 2 * dh), F32),
            ],
        ),
        compiler_params=_cparams(("arbitrary", "arbitrary")),
        name="diff_decode_attn",
    )(page_table, diff_lam, subln_g.reshape(-1, 1, 2 * dh), slopes, wq, kn, vn,
      *([kt_cache] * np_), *([v_cache] * np_))
    return out.reshape(n_seq, DIFF_Q_DIM)


def _swa_slopes():
    return 2.0 ** (-8.0 * np.arange(1, SWA_HEADS + 1, dtype=np.float32) / SWA_HEADS)


def _swa_prompt_kernel(sink_ref, slope_ref, q_ref, cur_ref, prev_ref, o_ref, *, n_blocks):
    w = WINDOW
    dh = SWA_HEAD_DIM
    bi = pl.program_id(0) % n_blocks
    krow = lax.broadcasted_iota(jnp.int32, (2 * w, w), 0)
    qcol = lax.broadcasted_iota(jnp.int32, (2 * w, w), 1)
    dist = (qcol - krow + w).astype(F32)
    in_seq = jnp.logical_or(krow >= w, (jnp.zeros_like(krow) + bi) > 0)
    ok = jnp.logical_and(jnp.logical_and(krow > qcol, krow <= qcol + w), in_seq)

    qt = (q_ref[...] * (dh ** -0.5)).T.astype(BF16)
    kk = jnp.concatenate([prev_ref[:, :SWA_KV_DIM], cur_ref[:, :SWA_KV_DIM]], axis=0).astype(BF16)
    vt = jnp.concatenate([prev_ref[:, SWA_KV_DIM:], cur_ref[:, SWA_KV_DIM:]], axis=0).T.astype(BF16)
    for kh in range(SWA_KV_HEADS):
        q_heads = jnp.concatenate(
            [qt[(kh * SWA_GROUP + g) * dh:(kh * SWA_GROUP + g + 1) * dh] for g in range(SWA_GROUP)],
            axis=1)
        s_all = jnp.dot(kk[:, kh * dh:(kh + 1) * dh], q_heads, preferred_element_type=F32)
        probs = []
        for g in range(SWA_GROUP):
            hd = kh * SWA_GROUP + g
            sink = sink_ref[hd]
            s = jnp.where(ok, s_all[:, g * w:(g + 1) * w] - slope_ref[hd] * dist, NEG_INF)
            m = jnp.maximum(jnp.max(s, axis=0, keepdims=True), sink)
            e = jnp.exp(s - m)
            den = jnp.sum(e, axis=0, keepdims=True) + jnp.exp(sink - m)
            probs.append((e * (1.0 / den)).astype(BF16))
        ot = jnp.dot(vt[kh * dh:(kh + 1) * dh], jnp.concatenate(probs, axis=1),
                     preferred_element_type=F32)
        for g in range(0, SWA_GROUP, 2):
            pair = jnp.concatenate([ot[:, g * w:(g + 1) * w], ot[:, (g + 1) * w:(g + 2) * w]], axis=0)
            hd = kh * SWA_GROUP + g
            o_ref[:, hd * dh:(hd + 2) * dh] = pair.T.astype(o_ref.dtype)


def swa_prompt_attention(h, sinks, n_batch, seq):
    nb = seq // WINDOW
    kvw = 2 * SWA_KV_DIM
    kv_col = SWA_Q_DIM // kvw
    slopes = jnp.asarray(_swa_slopes(), F32)
    smem = pl.BlockSpec(memory_space=pltpu.SMEM)
    return pl.pallas_call(
        functools.partial(_swa_prompt_kernel, n_blocks=nb),
        out_shape=jax.ShapeDtypeStruct((n_batch * seq, SWA_Q_DIM), BF16),
        grid=(n_batch * nb,),
        in_specs=[
            smem, smem,
            pl.BlockSpec((WINDOW, SWA_Q_DIM), lambda r: (r, 0)),
            pl.BlockSpec((WINDOW, kvw), lambda r: (r, kv_col)),
            pl.BlockSpec((WINDOW, kvw), lambda r: (jnp.maximum(r - 1, 0), kv_col)),
        ],
        out_specs=pl.BlockSpec((WINDOW, SWA_Q_DIM), lambda r: (r, 0)),
        compiler_params=_cparams(("arbitrary",)),
        name="swa_prompt_attn",
    )(sinks, slopes, h, h, h)


def _swa_decode_kernel(sink_ref, slope_ref, wq_ref, kn_ref, vn_ref, kt_ref, vt_ref, o_ref, *, win_buf):
    dh = SWA_HEAD_DIM
    lane = lax.broadcasted_iota(jnp.int32, (1, win_buf), 1)
    dist = (win_buf - lane).astype(F32)
    ok = dist < float(WINDOW)
    rowi = lax.broadcasted_iota(jnp.int32, (SWA_HEADS, 1), 0)
    is_kv0 = rowi < SWA_GROUP
    sink = sink_ref[...]
    slope = slope_ref[...]
    for sq in range(SWA_SEQS):
        wq = wq_ref[sq]
        kt = kt_ref[sq].astype(BF16)
        vt = vt_ref[sq].astype(BF16)
        s = jnp.dot(wq, kt, preferred_element_type=F32)
        s = jnp.where(ok, s - slope * dist, NEG_INF)
        s_n = jnp.sum(wq.astype(F32) * kn_ref[sq], axis=-1, keepdims=True)
        m = jnp.maximum(jnp.maximum(jnp.max(s, axis=-1, keepdims=True), s_n), sink)
        e = jnp.exp(s - m)
        e_n = jnp.exp(s_n - m)
        den = jnp.sum(e, axis=-1, keepdims=True) + e_n + jnp.exp(sink - m)
        inv = 1.0 / den
        pv = lax.dot_general((e * inv).astype(BF16), vt, (((1,), (1,)), ((), ())),
                             preferred_element_type=F32)
        pv = pv + (e_n * inv) * vn_ref[sq]
        o_ref[sq] = jnp.where(is_kv0, pv[:, :dh], pv[:, dh:]).astype(o_ref.dtype)


def swa_decode_attention(hs, kt_buf, vt_buf, sinks):
    n_seq = hs.shape[0]
    win_buf = kt_buf.shape[-1]
    dh = SWA_HEAD_DIM
    q = hs[:, :SWA_Q_DIM].reshape(n_seq, SWA_KV_HEADS, SWA_GROUP, dh) * (dh ** -0.5)
    eye = jnp.eye(SWA_KV_HEADS, dtype=F32)
    wq = jnp.einsum('bkgd,kK->bkgKd', q, eye).reshape(n_seq, SWA_HEADS, SWA_KV_DIM).astype(BF16)
    kn = hs[:, SWA_Q_DIM:SWA_Q_DIM + SWA_KV_DIM].reshape(n_seq, 1, SWA_KV_DIM)
    vn = hs[:, SWA_Q_DIM + SWA_KV_DIM:].reshape(n_seq, 1, SWA_KV_DIM)
    slopes = jnp.asarray(_swa_slopes().reshape(SWA_HEADS, 1), F32)
    sb = SWA_SEQS
    seq3 = lambda i: (i, 0, 0)
    out = pl.pallas_call(
        functools.partial(_swa_decode_kernel, win_buf=win_buf),
        out_shape=jax.ShapeDtypeStruct((n_seq, SWA_HEADS, dh), BF16),
        grid=(n_seq // sb,),
        in_specs=[
            pl.BlockSpec((SWA_HEADS, 1), lambda i: (0, 0)),
            pl.BlockSpec((SWA_HEADS, 1), lambda i: (0, 0)),
            pl.BlockSpec((sb, SWA_HEADS, SWA_KV_DIM), seq3),
            pl.BlockSpec((sb, 1, SWA_KV_DIM), seq3),
            pl.BlockSpec((sb, 1, SWA_KV_DIM), seq3),
            pl.BlockSpec((sb, SWA_KV_DIM, win_buf), seq3),
            pl.BlockSpec((sb, SWA_KV_DIM, win_buf), seq3),
        ],
        out_specs=pl.BlockSpec((sb, SWA_HEADS, dh), seq3),
        compiler_params=_cparams(("arbitrary",)),
        name="swa_decode_attn",
    )(sinks.reshape(SWA_HEADS, 1), slopes, wq, kn, vn, kt_buf, vt_buf)
    return out.reshape(n_seq, SWA_Q_DIM)


def _route_rows(x, w_ref, b_ref, e_ref, g_ref, r_ref, cnt_ref, carry_ref):
    tm = x.shape[0]

    @pl.when(pl.program_id(0) == 0)
    def _():
        carry_ref[...] = jnp.zeros(carry_ref.shape, F32)

    logits = jnp.dot(x.astype(BF16), w_ref[...].astype(BF16),
                     preferred_element_type=F32) + b_ref[...]
    lane = lax.broadcasted_iota(jnp.int32, (tm, N_EXPERTS), 1).astype(F32)
    work = logits
    vals, idxs = [], []
    for _ in range(TOP_K):
        mx = jnp.max(work, axis=-1, keepdims=True)
        ix = jnp.min(jnp.where(work == mx, lane, float(N_EXPERTS)), axis=-1, keepdims=True)
        vals.append(mx)
        idxs.append(ix)
        work = jnp.where(lane == ix, -jnp.inf, work)
    ex = [jnp.exp(v - vals[0]) for v in vals]
    den = ex[0] + ex[1] + ex[2] + ex[3]

    onehot = jnp.zeros((tm, N_EXPERTS), F32)
    for ix in idxs:
        onehot = onehot + (lane == ix).astype(F32)
    ri = lax.broadcasted_iota(jnp.int32, (tm, tm), 0)
    ci = lax.broadcasted_iota(jnp.int32, (tm, tm), 1)
    tri = (ci < ri).astype(BF16)
    before = jnp.dot(tri, onehot.astype(BF16), preferred_element_type=F32) + carry_ref[...]

    col4 = lax.broadcasted_iota(jnp.int32, (tm, TOP_K), 1)
    e_out = jnp.zeros((tm, TOP_K), jnp.int32)
    g_out = jnp.zeros((tm, TOP_K), F32)
    r_out = jnp.zeros((tm, TOP_K), jnp.int32)
    for k in range(TOP_K):
        rank = jnp.sum(jnp.where(lane == idxs[k], before, 0.0), axis=-1, keepdims=True)
        e_out = jnp.where(col4 == k, idxs[k].astype(jnp.int32), e_out)
        g_out = jnp.where(col4 == k, ex[k] / den, g_out)
        r_out = jnp.where(col4 == k, rank.astype(jnp.int32), r_out)
    e_ref[...] = e_out
    g_ref[...] = g_out
    r_ref[...] = r_out
    carry_ref[...] = carry_ref[...] + jnp.sum(onehot, axis=0, keepdims=True)
    cnt_ref[...] = carry_ref[...]


MAX_COPIES_PER_STEP = 2048


def _copies_per_step(n):
    return max(c for c in range(8, MAX_COPIES_PER_STEP + 1, 8) if n % c == 0)


def _token_copy(src_ref, dst_ref, sem, src_tok, dst_tok):
    return pltpu.make_async_copy(src_ref.at[src_tok], dst_ref.at[dst_tok], sem)


def _drain_token_copies(src_ref, dst_ref, sem, n):
    def drain(i, carry):
        _token_copy(src_ref, dst_ref, sem, 0, 0).wait()
        return carry

    lax.fori_loop(0, n, drain, 0, unroll=8)


def _scatter_tokens_kernel(start_ref, e_ref, r_ref, src_ref, out_ref, sem, *, n):
    def issue(h, carry):
        for prio in range(2):
            i = 2 * h + prio
            tok = lax.shift_right_logical(i, TOP_K.bit_length() - 1)
            slot = start_ref[e_ref[0, i]] + r_ref[0, i]
            _token_copy(src_ref, out_ref, sem, tok, slot).start(priority=prio)
        return carry

    lax.fori_loop(0, n // 2, issue, 0, unroll=4)
    _drain_token_copies(src_ref, out_ref, sem, n)


def scatter_tokens(src_tiles, top_e, rank, seg_start, n_out):
    t = src_tiles.shape[0] // TILE_ROWS
    n = _copies_per_step(t * TOP_K)
    steps = t * TOP_K // n
    idx_spec = pl.BlockSpec((None, 1, n), lambda s: (s, 0, 0), memory_space=pltpu.SMEM)
    out = pl.pallas_call(
        functools.partial(_scatter_tokens_kernel, n=n),
        out_shape=jax.ShapeDtypeStruct((n_out, TILE_ROWS, 128), F32),
        grid=(steps,),
        in_specs=[pl.BlockSpec(memory_space=pltpu.SMEM), idx_spec, idx_spec,
                  pl.BlockSpec((n // TOP_K, TILE_ROWS, 128), lambda s: (s, 0, 0))],
        out_specs=pl.BlockSpec(memory_space=pl.ANY),
        scratch_shapes=[pltpu.SemaphoreType.DMA(())],
        compiler_params=_cparams(("arbitrary",)),
        name="scatter_tokens",
    )(seg_start, top_e.reshape(steps, 1, n), rank.reshape(steps, 1, n),
      src_tiles.reshape(t, TILE_ROWS, 128))
    return out.reshape(n_out * TILE_ROWS, 128)


def _experts_kernel(be_ref, nx_ref, bv_ref, x_ref, wgu_hbm, bgu_ref, wd_hbm, bd_ref, o_ref,
                    wgu_f32, wd_f32, wgu_bf, wd_bf, sems, *, layer):
    s = pl.program_id(0)
    e = be_ref[s]
    valid = bv_ref[s]
    new_expert = jnp.logical_or(s == 0, e != be_ref[jnp.maximum(s - 1, 0)])
    used = valid > 0

    def weight_copies(expert):
        return (pltpu.make_async_copy(wgu_hbm.at[layer, expert], wgu_f32, sems.at[0]),
                pltpu.make_async_copy(wd_hbm.at[layer, expert], wd_f32, sems.at[1]))

    @pl.when(s == 0)
    def _():
        for cp in weight_copies(e):
            cp.start()

    @pl.when(jnp.logical_and(new_expert, used))
    def _():
        for cp in weight_copies(e):
            cp.wait()
        wgu_bf[...] = wgu_f32[...].astype(BF16)
        wd_bf[...] = wd_f32[...].astype(BF16)

        @pl.when(nx_ref[s] >= 0)
        def _():
            for cp in weight_copies(nx_ref[s]):
                cp.start()

    @pl.when(used)
    def _():
        rows = lax.broadcasted_iota(jnp.int32, (MOE_TM, 1), 0)
        x = jnp.where(rows < valid, _load_token_tiles(x_ref), 0.0).astype(BF16)
        h = jnp.dot(x, wgu_bf[...], preferred_element_type=F32) + bgu_ref[...]
        gate = jnp.minimum(h[:, :D_FF], SWIGLU_LIMIT)
        up = jnp.clip(h[:, D_FF:], -SWIGLU_LIMIT, SWIGLU_LIMIT)
        act = (up + 1.0) * (gate * jax.nn.sigmoid(SWIGLU_ALPHA * gate))
        y = jnp.dot(act.astype(BF16), wd_bf[...], preferred_element_type=F32) + bd_ref[...]
        _store_token_tiles(o_ref, y)

    @pl.when(jnp.logical_not(used))
    def _():
        o_ref[...] = jnp.zeros(o_ref.shape, o_ref.dtype)


def experts(xs, block_e, next_e, block_valid, w_gu, b_gu, w_down, b_down, layer):
    d = D_MODEL
    n_blocks = xs.shape[0] // (MOE_TM * TILE_ROWS)
    tile_block = pl.BlockSpec((MOE_TM * TILE_ROWS, 128), lambda s, be, nx, bv: (s, 0))
    any_spec = pl.BlockSpec(memory_space=pl.ANY)
    return pl.pallas_call(
        functools.partial(_experts_kernel, layer=layer),
        out_shape=jax.ShapeDtypeStruct(xs.shape, F32),
        grid_spec=pltpu.PrefetchScalarGridSpec(
            num_scalar_prefetch=3,
            grid=(n_blocks,),
            in_specs=[
                tile_block,
                any_spec,
                pl.BlockSpec((None, None, 1, 2 * D_FF), lambda s, be, nx, bv: (layer, be[s], 0, 0)),
                any_spec,
                pl.BlockSpec((None, None, 1, d), lambda s, be, nx, bv: (layer, be[s], 0, 0)),
            ],
            out_specs=tile_block,
            scratch_shapes=[
                pltpu.VMEM((d, 2 * D_FF), F32),
                pltpu.VMEM((D_FF, d), F32),
                pltpu.VMEM((d, 2 * D_FF), BF16),
                pltpu.VMEM((D_FF, d), BF16),
                pltpu.SemaphoreType.DMA((2,)),
            ],
        ),
        compiler_params=_cparams(("arbitrary",)),
        name="experts",
    )(block_e, next_e, block_valid, xs, w_gu, b_gu.reshape(DEPTH, N_EXPERTS, 1, 2 * D_FF),
      w_down, b_down.reshape(DEPTH, N_EXPERTS, 1, d))


def _combine_ln_kernel(start_ref, e0_ref, r0_ref, en_ref, rn_ref, ys_ref, x_ref, gt_ref, g_ref, b_ref,
                       o_ref, buf_ref, sems):
    s = pl.program_id(0)
    n_tiles = pl.num_programs(0)
    n = TOP_K * ROW_TILE

    def copy(slot, i, src_tok):
        dst = buf_ref.at[slot, pl.ds(pl.multiple_of(i * TILE_ROWS, TILE_ROWS), TILE_ROWS), :]
        return pltpu.make_async_copy(ys_ref.at[src_tok], dst, sems.at[slot])

    def fetch(slot, e_ref, r_ref):
        def issue(h, carry):
            for prio in range(2):
                i = 2 * h + prio
                copy(slot, i, start_ref[e_ref[0, i]] + r_ref[0, i]).start(priority=prio)
            return carry

        lax.fori_loop(0, n // 2, issue, 0, unroll=4)

    def drain(slot):
        def wait(i, carry):
            copy(slot, 0, 0).wait()
            return carry

        lax.fori_loop(0, n, wait, 0, unroll=8)

    @pl.when(s == 0)
    def _():
        fetch(0, e0_ref, r0_ref)

    for slot in range(2):
        @pl.when(s % 2 == slot)
        def _():
            @pl.when(s + 1 < n_tiles)
            def _():
                fetch(1 - slot, en_ref, rn_ref)

            drain(slot)
            gt = gt_ref[...]
            f = None
            for k in range(TOP_K):
                base = k * ROW_TILE * TILE_ROWS
                y = jnp.concatenate(
                    [buf_ref[slot, pl.ds(base + c, ROW_TILE, stride=TILE_ROWS), :] for c in range(TILE_ROWS)],
                    axis=-1) * gt[:, k:k + 1]
                f = y if f is None else f + y
            o_ref[...] = _layer_norm_rows(ALPHA * x_ref[...] + f, g_ref[...], b_ref[...])


def combine_ln(x, ys, top_e, rank, seg_start, gates, g_all, b_all, layer):
    t, d = x.shape
    nt = t // ROW_TILE
    n = TOP_K * ROW_TILE
    row = lambda i: (i, 0)
    by_tile = lambda a: a.reshape(nt, ROW_TILE, TOP_K).transpose(0, 2, 1).reshape(nt, 1, n)
    e_t, r_t = by_tile(top_e), by_tile(rank)
    first = pl.BlockSpec((None, 1, n), lambda i: (0, 0, 0), memory_space=pltpu.SMEM)
    ahead = pl.BlockSpec((None, 1, n), lambda i: (jnp.minimum(i + 1, nt - 1), 0, 0),
                         memory_space=pltpu.SMEM)
    return pl.pallas_call(
        _combine_ln_kernel,
        out_shape=jax.ShapeDtypeStruct((t, d), F32),
        grid=(nt,),
        in_specs=[
            pl.BlockSpec(memory_space=pltpu.SMEM),
            first, first, ahead, ahead,
            pl.BlockSpec(memory_space=pl.ANY),
            pl.BlockSpec((ROW_TILE, d), row),
            pl.BlockSpec((ROW_TILE, TOP_K), row),
            pl.BlockSpec((None, 1, d), lambda i: (layer, 0, 0)),
            pl.BlockSpec((None, 1, d), lambda i: (layer, 0, 0)),
        ],
        out_specs=pl.BlockSpec((ROW_TILE, d), row),
        scratch_shapes=[pltpu.VMEM((2, n * TILE_ROWS, 128), F32),
                        pltpu.SemaphoreType.DMA((2,))],
        compiler_params=_cparams(("arbitrary",)),
        name="combine_ln",
    )(seg_start, e_t, r_t, e_t, r_t, ys.reshape(-1, TILE_ROWS, 128), x, gates,
      g_all.reshape(-1, 1, d), b_all.reshape(-1, 1, d))


def moe_layer(x, x_tiles, top_e, gates, rank, counts, layer, w_gu, b_gu, w_down, b_down, ln_g, ln_b):
    t, d = x.shape
    counts = counts.reshape(N_EXPERTS).astype(jnp.int32)
    padded = (counts + MOE_TM - 1) // MOE_TM * MOE_TM
    pad_end = jnp.cumsum(padded)
    pad_start = pad_end - padded
    n_blocks = -(-t * TOP_K // MOE_TM) + N_EXPERTS
    block_start = jnp.arange(n_blocks, dtype=jnp.int32) * MOE_TM
    block_e = jnp.minimum(jnp.sum(pad_end[None, :] <= block_start[:, None], axis=1),
                          N_EXPERTS - 1).astype(jnp.int32)
    n_used = pad_end[-1] // MOE_TM
    block_valid = jnp.where(block_start < pad_end[-1],
                            jnp.clip(pad_start[block_e] + counts[block_e] - block_start, 0, MOE_TM),
                            0).astype(jnp.int32)
    after = pad_end[block_e] // MOE_TM
    next_e = jnp.where(after < n_used, block_e[jnp.minimum(after, n_blocks - 1)], -1).astype(jnp.int32)
    pad_start = pad_start.astype(jnp.int32)
    xs = scatter_tokens(x_tiles, top_e, rank, pad_start, n_blocks * MOE_TM)
    ys = experts(xs, block_e, next_e, block_valid, w_gu, b_gu, w_down, b_down, layer)
    return combine_ln(x, ys, top_e, rank, pad_start, gates, ln_g, ln_b, layer)


def kernel(x_prompt, x_sample, cache_k_diff, cache_v_diff, cache_k_swa, cache_v_swa, page_table,
           ln1_g, ln1_b, ln2_g, ln2_b, w_qkv_diff, diff_lam, diff_subln_g, w_o_diff,
           w_qkv_swa, b_qkv_swa, swa_sinks, w_o_swa, b_o_swa,
           w_router, b_router, w_gu, b_gu, w_down, b_down):
    n_batch, seq, d = x_prompt.shape
    n_seq = x_sample.shape[0]
    tp = n_batch * seq
    x = jnp.concatenate([x_prompt.reshape(tp, d), x_sample.reshape(n_seq, d)], axis=0)

    n_layers_diff, n_phys = cache_k_diff.shape[:2]
    kt_cache = jnp.transpose(cache_k_diff, (0, 1, 3, 4, 5, 2)).reshape(
        n_layers_diff, n_phys, DIFF_K_DIM, PAGE_SIZE)
    v_cache = cache_v_diff.reshape(n_layers_diff, n_phys, PAGE_SIZE * DIFF_KV_HEADS, 2 * DIFF_HEAD_DIM)
    win_buf = cache_k_swa.shape[2]
    kt_swa = jnp.transpose(cache_k_swa, (0, 1, 3, 4, 2)).reshape(-1, n_seq, SWA_KV_DIM, win_buf)
    vt_swa = jnp.transpose(cache_v_swa, (0, 1, 3, 4, 2)).reshape(-1, n_seq, SWA_KV_DIM, win_buf)

    zero_bias_qkv = jnp.zeros((1, w_qkv_diff.shape[-1]), F32)
    zero_bias_o = jnp.zeros((1, d), F32)

    kdp, vdp, kds, vds = [], [], [], []
    ksp, vsp, kss, vss = [], [], [], []
    for i in range(DEPTH):
        j = i // 2
        if i % 2 == 0:
            kq, vq = DIFF_Q_DIM, DIFF_Q_DIM + DIFF_K_DIM
            h, (k_new, v_new) = linear(x, w_qkv_diff, j, zero_bias_qkv,
                                       col_ranges=((kq, vq), (vq, vq + DIFF_V_DIM)))
            hs = h[tp:]
            a_p = diff_prompt_attention(h, diff_lam, diff_subln_g, j, i, n_batch, seq)
            a_s = diff_decode_attention(hs, kt_cache, v_cache, page_table, diff_lam, diff_subln_g, j, i)
            kdp.append(k_new[:tp].reshape(n_batch, seq, DIFF_KV_HEADS, 2, DIFF_HEAD_DIM))
            vdp.append(v_new[:tp].reshape(n_batch, seq, DIFF_KV_HEADS, 2 * DIFF_HEAD_DIM))
            kds.append(k_new[tp:].reshape(n_seq, 1, DIFF_KV_HEADS, 2, DIFF_HEAD_DIM))
            vds.append(v_new[tp:].reshape(n_seq, 1, DIFF_KV_HEADS, 2 * DIFF_HEAD_DIM))
            a = jnp.concatenate([a_p, a_s], axis=0)
            x, x_tiles, *routing = oproj_ln_route(a, w_o_diff, j, zero_bias_o, x, ln1_g, ln1_b,
                                                  w_router, b_router, i)
        else:
            h, _ = linear(x, w_qkv_swa, j, b_qkv_swa[j].reshape(1, -1))
            hs = h[tp:]
            a_p = swa_prompt_attention(h, swa_sinks[j], n_batch, seq)
            a_s = swa_decode_attention(hs, kt_swa[j], vt_swa[j], swa_sinks[j])
            kq, vq = SWA_Q_DIM, SWA_Q_DIM + SWA_KV_DIM
            keep = min(WINDOW, seq)
            hp = h[:tp].reshape(n_batch, seq, -1)[:, seq - keep:]
            ksp.append(hp[..., kq:vq].reshape(n_batch, keep, SWA_KV_HEADS, SWA_HEAD_DIM))
            vsp.append(hp[..., vq:].reshape(n_batch, keep, SWA_KV_HEADS, SWA_HEAD_DIM))
            k_new = hs[:, kq:vq].reshape(n_seq, 1, SWA_KV_HEADS, SWA_HEAD_DIM)
            v_new = hs[:, vq:].reshape(n_seq, 1, SWA_KV_HEADS, SWA_HEAD_DIM)
            kss.append(jnp.concatenate([cache_k_swa[j], k_new], axis=1)[:, -win_buf:])
            vss.append(jnp.concatenate([cache_v_swa[j], v_new], axis=1)[:, -win_buf:])
            a = jnp.concatenate([a_p, a_s], axis=0)
            x, x_tiles, *routing = oproj_ln_route(a, w_o_swa, j, b_o_swa[j].reshape(1, -1), x, ln1_g, ln1_b,
                                                  w_router, b_router, i)
        x = moe_layer(x, x_tiles, *routing, i, w_gu, b_gu, w_down, b_down, ln2_g, ln2_b)

    return (x[:tp].reshape(n_batch, seq, d), x[tp:].reshape(n_seq, 1, d),
            jnp.stack(kdp), jnp.stack(vdp), jnp.stack(kds), jnp.stack(vds),
            jnp.stack(ksp), jnp.stack(vsp), jnp.stack(kss), jnp.stack(vss))
```

```python
import functools
import math

import jax
import jax.numpy as jnp
import numpy as np
from jax import lax
from jax.experimental import pallas as pl
from jax.experimental.pallas import tpu as pltpu

F32 = jnp.float32
BF16 = jnp.bfloat16

D_MODEL = 1024
DEPTH = 4
PAGE_SIZE = 128
DIFF_HEAD_DIM = 64
DIFF_HEADS = 8
DIFF_KV_HEADS = 2
DIFF_GROUP = 4
DIFF_Q_DIM = 1024
DIFF_K_DIM = 256
DIFF_V_DIM = 256
SWA_HEAD_DIM = 64
SWA_HEADS = 16
SWA_KV_HEADS = 2
SWA_GROUP = 8
SWA_Q_DIM = 1024
SWA_KV_DIM = 128
WINDOW = 128
N_EXPERTS = 32
TOP_K = 4
D_FF = 1024
SWIGLU_ALPHA = 1.702
SWIGLU_LIMIT = 7.0
ALPHA = (2 * DEPTH) ** 0.25
LN_EPS = 1e-5
NEG_INF = -1e30

ROW_TILE = 384
DIFF_TQ = 256
MOE_TM = 512
DECODE_PAGES = 16
SWA_SEQS = 8
VMEM_LIMIT = 56 * 1024 * 1024


def _lambda_init(layer):
    return 0.8 - 0.6 * math.exp(-0.3 * layer)


def _cparams(sem):
    return pltpu.CompilerParams(dimension_semantics=sem, vmem_limit_bytes=VMEM_LIMIT)


def _linear_kernel(x_ref, w_ref, b_ref, o_ref, *rest, col_ranges):
    col_refs, wbf_ref = rest[:-1], rest[-1]

    @pl.when(pl.program_id(0) == 0)
    def _():
        wbf_ref[...] = w_ref[...].astype(BF16)

    acc = jnp.dot(x_ref[...].astype(BF16), wbf_ref[...], preferred_element_type=F32) + b_ref[...]
    o_ref[...] = acc
    for (lo, hi), ref in zip(col_ranges, col_refs):
        ref[...] = acc[:, lo:hi]


def linear(x, w_all, layer, bias, col_ranges=()):
    t, k = x.shape
    n = w_all.shape[-1]
    row = lambda i: (i, 0)
    outs = pl.pallas_call(
        functools.partial(_linear_kernel, col_ranges=tuple(col_ranges)),
        out_shape=(jax.ShapeDtypeStruct((t, n), F32),
                   *[jax.ShapeDtypeStruct((t, hi - lo), F32) for lo, hi in col_ranges]),
        grid=(t // ROW_TILE,),
        in_specs=[
            pl.BlockSpec((ROW_TILE, k), row),
            pl.BlockSpec((None, k, n), lambda i: (layer, 0, 0)),
            pl.BlockSpec((1, n), lambda i: (0, 0)),
        ],
        out_specs=(pl.BlockSpec((ROW_TILE, n), row),
                   *[pl.BlockSpec((ROW_TILE, hi - lo), row) for lo, hi in col_ranges]),
        scratch_shapes=[pltpu.VMEM((k, n), BF16)],
        compiler_params=_cparams(("arbitrary",)),
        name="linear",
    )(x, w_all, bias)
    return outs[0], outs[1:]


def _layer_norm_rows(z, g, b):
    mu = jnp.mean(z, axis=-1, keepdims=True)
    zc = z - mu
    var = jnp.mean(zc * zc, axis=-1, keepdims=True)
    return zc * lax.rsqrt(var + LN_EPS) * g + b


TILE_ROWS = D_MODEL // 128


def _store_token_tiles(ref, val):
    n = val.shape[0]
    for c in range(TILE_ROWS):
        ref[pl.ds(c, n, stride=TILE_ROWS), :] = val[:, c * 128:(c + 1) * 128]


def _load_token_tiles(ref):
    n = ref.shape[0] // TILE_ROWS
    return jnp.concatenate([ref[pl.ds(c, n, stride=TILE_ROWS), :] for c in range(TILE_ROWS)], axis=-1)


def _oproj_ln_kernel(a_ref, w_ref, bo_ref, x_ref, g_ref, b_ref, wr_ref, br_ref,
                     o_ref, ot_ref, e_ref, gt_ref, r_ref, cnt_ref, wbf_ref, carry_ref):
    @pl.when(pl.program_id(0) == 0)
    def _():
        wbf_ref[...] = w_ref[...].astype(BF16)

    y = jnp.dot(a_ref[...], wbf_ref[...], preferred_element_type=F32) + bo_ref[...]
    z = _layer_norm_rows(ALPHA * x_ref[...] + y, g_ref[...], b_ref[...])
    o_ref[...] = z
    _store_token_tiles(ot_ref, z)
    _route_rows(z, wr_ref, br_ref, e_ref, gt_ref, r_ref, cnt_ref, carry_ref)


def oproj_ln_route(a, w_all, layer_slot, b_o, x, g_all, b_all, w_router, b_router, layer):
    t, d = x.shape
    row = lambda i: (i, 0)
    return pl.pallas_call(
        _oproj_ln_kernel,
        out_shape=(jax.ShapeDtypeStruct((t, d), F32),
                   jax.ShapeDtypeStruct((t * TILE_ROWS, 128), F32),
                   jax.ShapeDtypeStruct((t, TOP_K), jnp.int32),
                   jax.ShapeDtypeStruct((t, TOP_K), F32),
                   jax.ShapeDtypeStruct((t, TOP_K), jnp.int32),
                   jax.ShapeDtypeStruct((1, N_EXPERTS), F32)),
        grid=(t // ROW_TILE,),
        in_specs=[
            pl.BlockSpec((ROW_TILE, d), row),
            pl.BlockSpec((None, d, d), lambda i: (layer_slot, 0, 0)),
            pl.BlockSpec((1, d), lambda i: (0, 0)),
            pl.BlockSpec((ROW_TILE, d), row),
            pl.BlockSpec((None, 1, d), lambda i: (layer, 0, 0)),
            pl.BlockSpec((None, 1, d), lambda i: (layer, 0, 0)),
            pl.BlockSpec((None, d, N_EXPERTS), lambda i: (layer, 0, 0)),
            pl.BlockSpec((None, 1, N_EXPERTS), lambda i: (layer, 0, 0)),
        ],
        out_specs=(pl.BlockSpec((ROW_TILE, d), row),
                   pl.BlockSpec((ROW_TILE * TILE_ROWS, 128), row),
                   pl.BlockSpec((ROW_TILE, TOP_K), row),
                   pl.BlockSpec((ROW_TILE, TOP_K), row),
                   pl.BlockSpec((ROW_TILE, TOP_K), row),
                   pl.BlockSpec((1, N_EXPERTS), lambda i: (0, 0))),
        scratch_shapes=[pltpu.VMEM((d, d), BF16), pltpu.VMEM((1, N_EXPERTS), F32)],
        compiler_params=_cparams(("arbitrary",)),
        name="oproj_ln_route",
    )(a, w_all, b_o, x, g_all.reshape(-1, 1, d), b_all.reshape(-1, 1, d),
      w_router, b_router.reshape(-1, 1, N_EXPERTS))


def _diff_prompt_kernel(lam_ref, sg_ref, q_ref, k_ref, v_ref, o_ref,
                        qt_ref, m_ref, l_ref, acc_ref, *, layer):
    tq = DIFF_TQ
    dh = DIFF_HEAD_DIM
    dv = 2 * dh
    kvh = pl.program_id(1)
    qi = pl.program_id(2)

    qt = (q_ref[...] * (dh ** -0.5)).T
    arow = lax.broadcasted_iota(jnp.int32, (dh, tq), 0)
    q_off = lax.broadcasted_iota(jnp.int32, (dh, tq), 1).astype(F32)
    for g in range(DIFF_GROUP):
        slope = jnp.where(kvh == 0, 2.0 ** -(g + 1), 2.0 ** -(DIFF_GROUP + g + 1)).astype(F32)
        extra = jnp.where(arow < 2, slope, jnp.where(arow == 2, -slope * q_off, 0.0))
        q0 = qt[(g * 2) * dh:(g * 2 + 1) * dh]
        q1 = qt[(g * 2 + 1) * dh:(g * 2 + 2) * dh]
        qt_ref[0, :, g * tq:(g + 1) * tq] = jnp.concatenate([q0, extra], axis=0).astype(BF16)
        qt_ref[1, :, g * tq:(g + 1) * tq] = jnp.concatenate([extra, q1], axis=0).astype(BF16)
    m_ref[...] = jnp.full(m_ref.shape, NEG_INF, F32)
    l_ref[...] = jnp.zeros(l_ref.shape, F32)
    acc_ref[...] = jnp.zeros(acc_ref.shape, F32)

    krow = lax.broadcasted_iota(jnp.int32, (tq, tq), 0)
    qcol = lax.broadcasted_iota(jnp.int32, (tq, tq), 1)
    causal = krow <= qcol

    def block(j, nk, masked):
        start = pl.multiple_of(j * tq, tq)
        kb = k_ref[pl.ds(start, nk), :]
        vt = v_ref[pl.ds(start, nk), :].T.astype(BF16)
        lane = lax.broadcasted_iota(jnp.int32, (nk, dv), 1)
        row = lax.broadcasted_iota(jnp.int32, (nk, dv), 0)
        k_off = jnp.bitwise_and(row, tq - 1).astype(F32)
        shift = ((j - qi) * tq).astype(F32) + jnp.bitwise_and(row, -tq).astype(F32)
        keys = [
            jnp.where(lane < dh, kb,
                      jnp.where(lane == dh, k_off,
                                jnp.where(lane == dh + 1, shift, jnp.where(lane == dh + 2, 1.0, 0.0)))),
            jnp.where(lane >= dh, kb,
                      jnp.where(lane == 0, k_off,
                                jnp.where(lane == 1, shift, jnp.where(lane == 2, 1.0, 0.0)))),
        ]
        for c in range(2):
            s_all = jnp.dot(keys[c].astype(BF16), qt_ref[c], preferred_element_type=F32)
            for g in range(DIFF_GROUP):
                idx = g * 2 + c
                s = s_all[:, g * tq:(g + 1) * tq]
                if masked:
                    s = jnp.where(causal, s, NEG_INF)
                m_old = m_ref[idx:idx + 1, :]
                m_new = jnp.maximum(m_old, jnp.max(s, axis=0, keepdims=True))
                alpha = jnp.exp(m_old - m_new)
                p = jnp.exp(s - m_new)
                l_ref[idx:idx + 1, :] = alpha * l_ref[idx:idx + 1, :] + jnp.sum(p, axis=0, keepdims=True)
                acc_ref[idx] = alpha * acc_ref[idx] + jnp.dot(vt, p.astype(BF16),
                                                              preferred_element_type=F32)
                m_ref[idx:idx + 1, :] = m_new

    def pair_body(p, carry):
        block(2 * p, 2 * tq, False)
        return carry

    lax.fori_loop(0, qi // 2, pair_body, 0)

    @pl.when(qi % 2 == 1)
    def _():
        block(qi - 1, tq, False)

    block(qi, tq, True)

    lp = lam_ref[...]
    lam = (jnp.exp(jnp.sum(lp[0:1] * lp[1:2], axis=-1, keepdims=True))
           - jnp.exp(jnp.sum(lp[2:3] * lp[3:4], axis=-1, keepdims=True)) + _lambda_init(layer))
    gain = sg_ref[...] * (1.0 - _lambda_init(layer))
    for g in range(DIFF_GROUP):
        o1 = acc_ref[g * 2] * (1.0 / l_ref[g * 2:g * 2 + 1, :])
        o2 = acc_ref[g * 2 + 1] * (1.0 / l_ref[g * 2 + 1:g * 2 + 2, :])
        ot = o1 - lam * o2
        ot = ot * lax.rsqrt(jnp.mean(ot * ot, axis=0, keepdims=True) + LN_EPS)
        o_ref[:, g * dv:(g + 1) * dv] = (ot.T * gain).astype(o_ref.dtype)


def diff_prompt_attention(h, diff_lam, subln_g, slot, layer, n_batch, seq):
    tq = DIFF_TQ
    nq = seq // tq
    qw = DIFF_GROUP * 2 * DIFF_HEAD_DIM
    kw = 2 * DIFF_HEAD_DIM
    return pl.pallas_call(
        functools.partial(_diff_prompt_kernel, layer=layer),
        out_shape=jax.ShapeDtypeStruct((n_batch * seq, DIFF_Q_DIM), BF16),
        grid=(n_batch, DIFF_KV_HEADS, nq),
        in_specs=[
            pl.BlockSpec((None, 4, DIFF_HEAD_DIM), lambda b, h_, i: (slot, 0, 0)),
            pl.BlockSpec((None, 1, kw), lambda b, h_, i: (slot, 0, 0)),
            pl.BlockSpec((tq, qw), lambda b, h_, i: (b * nq + i, h_)),
            pl.BlockSpec((seq, kw), lambda b, h_, i: (b, DIFF_Q_DIM // kw + h_)),
            pl.BlockSpec((seq, kw), lambda b, h_, i: (b, (DIFF_Q_DIM + DIFF_K_DIM) // kw + h_)),
        ],
        out_specs=pl.BlockSpec((tq, qw), lambda b, h_, i: (b * nq + i, h_)),
        scratch_shapes=[
            pltpu.VMEM((2, 2 * DIFF_HEAD_DIM, DIFF_GROUP * tq), BF16),
            pltpu.VMEM((2 * DIFF_GROUP, tq), F32),
            pltpu.VMEM((2 * DIFF_GROUP, tq), F32),
            pltpu.VMEM((2 * DIFF_GROUP, 2 * DIFF_HEAD_DIM, tq), F32),
        ],
        compiler_params=_cparams(("arbitrary", "arbitrary", "arbitrary")),
        name="diff_prompt_attn",
    )(diff_lam, subln_g.reshape(-1, 1, kw), h, h, h)


def _diff_decode_kernel(pt_ref, lam_ref, sg_ref, slope_ref, wq_ref, kn_ref, vn_ref, *rest,
                        layer, n_pages, past_len):
    del pt_ref
    np_ = DECODE_PAGES
    k_refs = rest[:np_]
    v_refs = rest[np_:2 * np_]
    o_ref = rest[2 * np_]
    m_ref, l_ref, acc_ref = rest[2 * np_ + 1:]
    jc = pl.program_id(1)
    n_chunks = n_pages // np_
    dv = 2 * DIFF_HEAD_DIM

    @pl.when(jc == 0)
    def _():
        m_ref[...] = jnp.full(m_ref.shape, NEG_INF, F32)
        l_ref[...] = jnp.zeros(l_ref.shape, F32)
        acc_ref[...] = jnp.zeros(acc_ref.shape, F32)

    wq = wq_ref[...]
    slope = slope_ref[...]
    lane = lax.broadcasted_iota(jnp.int32, (1, PAGE_SIZE), 1)
    rowi = lax.broadcasted_iota(jnp.int32, (16, 1), 0)
    is_kv0 = (rowi % (2 * DIFF_GROUP)) < DIFF_GROUP

    s_pages = []
    for g in range(np_):
        kt = k_refs[g][...].astype(BF16)
        s = jnp.dot(wq, kt, preferred_element_type=F32)
        k_pos = (jc * np_ + g) * PAGE_SIZE + lane
        s_pages.append(s - slope * (past_len - k_pos).astype(F32))
    m_old = m_ref[...]
    m_new = m_old
    for s in s_pages:
        m_new = jnp.maximum(m_new, jnp.max(s, axis=-1, keepdims=True))
    alpha = jnp.exp(m_old - m_new)
    l_new = alpha * l_ref[...]
    acc = alpha * acc_ref[...]
    for g in range(np_):
        p = jnp.exp(s_pages[g] - m_new)
        l_new = l_new + jnp.sum(p, axis=-1, keepdims=True)
        pb = p.astype(BF16)
        v0 = v_refs[g][pl.ds(0, PAGE_SIZE, stride=DIFF_KV_HEADS), :].astype(BF16)
        v1 = v_refs[g][pl.ds(1, PAGE_SIZE, stride=DIFF_KV_HEADS), :].astype(BF16)
        acc = acc + jnp.where(is_kv0, jnp.dot(pb, v0, preferred_element_type=F32),
                              jnp.dot(pb, v1, preferred_element_type=F32))
    m_ref[...] = m_new
    l_ref[...] = l_new
    acc_ref[...] = acc

    @pl.when(jc == n_chunks - 1)
    def _():
        kn = kn_ref[...]
        vn = vn_ref[...]
        s_n = jnp.sum(wq.astype(F32) * kn, axis=-1, keepdims=True)
        m_o = m_ref[...]
        m_f = jnp.maximum(m_o, s_n)
        a_f = jnp.exp(m_o - m_f)
        p_n = jnp.exp(s_n - m_f)
        l_f = a_f * l_ref[...] + p_n
        v_rows = jnp.where(is_kv0, vn[:, :dv], vn[:, dv:])
        acc_f = a_f * acc_ref[...] + p_n * v_rows
        o_maps = acc_f / l_f
        lp = lam_ref[...]
        lam = (jnp.exp(jnp.sum(lp[0:1] * lp[1:2], axis=-1, keepdims=True))
               - jnp.exp(jnp.sum(lp[2:3] * lp[3:4], axis=-1, keepdims=True)) + _lambda_init(layer))
        o = o_maps[:DIFF_HEADS] - lam * o_maps[DIFF_HEADS:]
        gain = sg_ref[...] * (1.0 - _lambda_init(layer))
        o = o * lax.rsqrt(jnp.mean(o * o, axis=-1, keepdims=True) + LN_EPS) * gain
        o_ref[...] = o.astype(o_ref.dtype)


def diff_decode_attention(hs, kt_cache, v_cache, page_table, diff_lam, subln_g, slot, layer):
    n_seq = hs.shape[0]
    n_pages = page_table.shape[1]
    past_len = n_pages * PAGE_SIZE
    dh = DIFF_HEAD_DIM
    q = hs[:, :DIFF_Q_DIM].reshape(n_seq, DIFF_KV_HEADS, DIFF_GROUP, 2, dh) * (dh ** -0.5)
    eye = jnp.eye(2, dtype=F32)
    wq = jnp.einsum('bkgcd,kK,cC->bckgKCd', q, eye, eye).reshape(n_seq, 16, DIFF_K_DIM).astype(BF16)
    kn = hs[:, DIFF_Q_DIM:DIFF_Q_DIM + DIFF_K_DIM].reshape(n_seq, 1, DIFF_K_DIM)
    vn = hs[:, DIFF_Q_DIM + DIFF_K_DIM:].reshape(n_seq, 1, DIFF_V_DIM)
    heads = np.arange(DIFF_HEADS, dtype=np.float32).reshape(DIFF_KV_HEADS, DIFF_GROUP)
    slopes = np.tile((2.0 ** -(heads + 1.0)).reshape(1, DIFF_HEADS), (2, 1)).reshape(16, 1)
    slopes = jnp.asarray(slopes, F32)

    np_ = DECODE_PAGES
    n_chunks = n_pages // np_

    def page_map(g):
        return lambda b, jc, pt: (slot, pt[b, jc * np_ + g], 0, 0)

    in_specs = [
        pl.BlockSpec((None, 4, dh), lambda b, jc, pt: (slot, 0, 0)),
        pl.BlockSpec((None, 1, 2 * dh), lambda b, jc, pt: (slot, 0, 0)),
        pl.BlockSpec((16, 1), lambda b, jc, pt: (0, 0)),
        pl.BlockSpec((None, 16, DIFF_K_DIM), lambda b, jc, pt: (b, 0, 0)),
        pl.BlockSpec((None, 1, DIFF_K_DIM), lambda b, jc, pt: (b, 0, 0)),
        pl.BlockSpec((None, 1, DIFF_V_DIM), lambda b, jc, pt: (b, 0, 0)),
    ]
    in_specs += [pl.BlockSpec((None, None, DIFF_K_DIM, PAGE_SIZE), page_map(g)) for g in range(np_)]
    in_specs += [pl.BlockSpec((None, None, PAGE_SIZE * DIFF_KV_HEADS, 2 * dh), page_map(g))
                 for g in range(np_)]
    out = pl.pallas_call(
        functools.partial(_diff_decode_kernel, layer=layer, n_pages=n_pages, past_len=past_len),
        out_shape=jax.ShapeDtypeStruct((n_seq, DIFF_HEADS, 2 * dh), BF16),
        grid_spec=pltpu.PrefetchScalarGridSpec(
            num_scalar_prefetch=1,
            grid=(n_seq, n_chunks),
            in_specs=in_specs,
            out_specs=pl.BlockSpec((None, DIFF_HEADS, 2 * dh), lambda b, jc, pt: (b, 0, 0)),
            scratch_shapes=[
                pltpu.VMEM((16, 1), F32),
                pltpu.VMEM((16, 1), F32),
                pltpu.VMEM((16, 2 * dh), F32),
            ],
        ),
        compiler_params=_cparams(("arbitrary", "arbitrary")),
        name="diff_decode_attn",
    )(page_table, diff_lam, subln_g.reshape(-1, 1, 2 * dh), slopes, wq, kn, vn,
      *([kt_cache] * np_), *([v_cache] * np_))
    return out.reshape(n_seq, DIFF_Q_DIM)


def _swa_slopes():
    return 2.0 ** (-8.0 * np.arange(1, SWA_HEADS + 1, dtype=np.float32) / SWA_HEADS)


def _swa_prompt_kernel(sink_ref, slope_ref, q_ref, cur_ref, prev_ref, o_ref, *, n_blocks):
    w = WINDOW
    dh = SWA_HEAD_DIM
    bi = pl.program_id(0) % n_blocks
    krow = lax.broadcasted_iota(jnp.int32, (2 * w, w), 0)
    qcol = lax.broadcasted_iota(jnp.int32, (2 * w, w), 1)
    dist = (qcol - krow + w).astype(F32)
    in_seq = jnp.logical_or(krow >= w, (jnp.zeros_like(krow) + bi) > 0)
    ok = jnp.logical_and(jnp.logical_and(krow > qcol, krow <= qcol + w), in_seq)

    qt = (q_ref[...] * (dh ** -0.5)).T.astype(BF16)
    kk = jnp.concatenate([prev_ref[:, :SWA_KV_DIM], cur_ref[:, :SWA_KV_DIM]], axis=0).astype(BF16)
    vt = jnp.concatenate([prev_ref[:, SWA_KV_DIM:], cur_ref[:, SWA_KV_DIM:]], axis=0).T.astype(BF16)
    for kh in range(SWA_KV_HEADS):
        q_heads = jnp.concatenate(
            [qt[(kh * SWA_GROUP + g) * dh:(kh * SWA_GROUP + g + 1) * dh] for g in range(SWA_GROUP)],
            axis=1)
        s_all = jnp.dot(kk[:, kh * dh:(kh + 1) * dh], q_heads, preferred_element_type=F32)
        probs = []
        for g in range(SWA_GROUP):
            hd = kh * SWA_GROUP + g
            sink = sink_ref[hd]
            s = jnp.where(ok, s_all[:, g * w:(g + 1) * w] - slope_ref[hd] * dist, NEG_INF)
            m = jnp.maximum(jnp.max(s, axis=0, keepdims=True), sink)
            e = jnp.exp(s - m)
            den = jnp.sum(e, axis=0, keepdims=True) + jnp.exp(sink - m)
            probs.append((e * (1.0 / den)).astype(BF16))
        ot = jnp.dot(vt[kh * dh:(kh + 1) * dh], jnp.concatenate(probs, axis=1),
                     preferred_element_type=F32)
        for g in range(0, SWA_GROUP, 2):
            pair = jnp.concatenate([ot[:, g * w:(g + 1) * w], ot[:, (g + 1) * w:(g + 2) * w]], axis=0)
            hd = kh * SWA_GROUP + g
            o_ref[:, hd * dh:(hd + 2) * dh] = pair.T.astype(o_ref.dtype)


def swa_prompt_attention(h, sinks, n_batch, seq):
    nb = seq // WINDOW
    kvw = 2 * SWA_KV_DIM
    kv_col = SWA_Q_DIM // kvw
    slopes = jnp.asarray(_swa_slopes(), F32)
    smem = pl.BlockSpec(memory_space=pltpu.SMEM)
    return pl.pallas_call(
        functools.partial(_swa_prompt_kernel, n_blocks=nb),
        out_shape=jax.ShapeDtypeStruct((n_batch * seq, SWA_Q_DIM), BF16),
        grid=(n_batch * nb,),
        in_specs=[
            smem, smem,
            pl.BlockSpec((WINDOW, SWA_Q_DIM), lambda r: (r, 0)),
            pl.BlockSpec((WINDOW, kvw), lambda r: (r, kv_col)),
            pl.BlockSpec((WINDOW, kvw), lambda r: (jnp.maximum(r - 1, 0), kv_col)),
        ],
        out_specs=pl.BlockSpec((WINDOW, SWA_Q_DIM), lambda r: (r, 0)),
        compiler_params=_cparams(("arbitrary",)),
        name="swa_prompt_attn",
    )(sinks, slopes, h, h, h)


def _swa_decode_kernel(sink_ref, slope_ref, wq_ref, kn_ref, vn_ref, kt_ref, vt_ref, o_ref, *, win_buf):
    dh = SWA_HEAD_DIM
    lane = lax.broadcasted_iota(jnp.int32, (1, win_buf), 1)
    dist = (win_buf - lane).astype(F32)
    ok = dist < float(WINDOW)
    rowi = lax.broadcasted_iota(jnp.int32, (SWA_HEADS, 1), 0)
    is_kv0 = rowi < SWA_GROUP
    sink = sink_ref[...]
    slope = slope_ref[...]
    for sq in range(SWA_SEQS):
        wq = wq_ref[sq]
        kt = kt_ref[sq].astype(BF16)
        vt = vt_ref[sq].astype(BF16)
        s = jnp.dot(wq, kt, preferred_element_type=F32)
        s = jnp.where(ok, s - slope * dist, NEG_INF)
        s_n = jnp.sum(wq.astype(F32) * kn_ref[sq], axis=-1, keepdims=True)
        m = jnp.maximum(jnp.maximum(jnp.max(s, axis=-1, keepdims=True), s_n), sink)
        e = jnp.exp(s - m)
        e_n = jnp.exp(s_n - m)
        den = jnp.sum(e, axis=-1, keepdims=True) + e_n + jnp.exp(sink - m)
        inv = 1.0 / den
        pv = lax.dot_general((e * inv).astype(BF16), vt, (((1,), (1,)), ((), ())),
                             preferred_element_type=F32)
        pv = pv + (e_n * inv) * vn_ref[sq]
        o_ref[sq] = jnp.where(is_kv0, pv[:, :dh], pv[:, dh:]).astype(o_ref.dtype)


def swa_decode_attention(hs, kt_buf, vt_buf, sinks):
    n_seq = hs.shape[0]
    win_buf = kt_buf.shape[-1]
    dh = SWA_HEAD_DIM
    q = hs[:, :SWA_Q_DIM].reshape(n_seq, SWA_KV_HEADS, SWA_GROUP, dh) * (dh ** -0.5)
    eye = jnp.eye(SWA_KV_HEADS, dtype=F32)
    wq = jnp.einsum('bkgd,kK->bkgKd', q, eye).reshape(n_seq, SWA_HEADS, SWA_KV_DIM).astype(BF16)
    kn = hs[:, SWA_Q_DIM:SWA_Q_DIM + SWA_KV_DIM].reshape(n_seq, 1, SWA_KV_DIM)
    vn = hs[:, SWA_Q_DIM + SWA_KV_DIM:].reshape(n_seq, 1, SWA_KV_DIM)
    slopes = jnp.asarray(_swa_slopes().reshape(SWA_HEADS, 1), F32)
    sb = SWA_SEQS
    seq3 = lambda i: (i, 0, 0)
    out = pl.pallas_call(
        functools.partial(_swa_decode_kernel, win_buf=win_buf),
        out_shape=jax.ShapeDtypeStruct((n_seq, SWA_HEADS, dh), BF16),
        grid=(n_seq // sb,),
        in_specs=[
            pl.BlockSpec((SWA_HEADS, 1), lambda i: (0, 0)),
            pl.BlockSpec((SWA_HEADS, 1), lambda i: (0, 0)),
            pl.BlockSpec((sb, SWA_HEADS, SWA_KV_DIM), seq3),
            pl.BlockSpec((sb, 1, SWA_KV_DIM), seq3),
            pl.BlockSpec((sb, 1, SWA_KV_DIM), seq3),
            pl.BlockSpec((sb, SWA_KV_DIM, win_buf), seq3),
            pl.BlockSpec((sb, SWA_KV_DIM, win_buf), seq3),
        ],
        out_specs=pl.BlockSpec((sb, SWA_HEADS, dh), seq3),
        compiler_params=_cparams(("arbitrary",)),
        name="swa_decode_attn",
    )(sinks.reshape(SWA_HEADS, 1), slopes, wq, kn, vn, kt_buf, vt_buf)
    return out.reshape(n_seq, SWA_Q_DIM)


def _route_rows(x, w_ref, b_ref, e_ref, g_ref, r_ref, cnt_ref, carry_ref):
    tm = x.shape[0]

    @pl.when(pl.program_id(0) == 0)
    def _():
        carry_ref[...] = jnp.zeros(carry_ref.shape, F32)

    logits = jnp.dot(x.astype(BF16), w_ref[...].astype(BF16),
                     preferred_element_type=F32) + b_ref[...]
    lane = lax.broadcasted_iota(jnp.int32, (tm, N_EXPERTS), 1).astype(F32)
    work = logits
    vals, idxs = [], []
    for _ in range(TOP_K):
        mx = jnp.max(work, axis=-1, keepdims=True)
        ix = jnp.min(jnp.where(work == mx, lane, float(N_EXPERTS)), axis=-1, keepdims=True)
        vals.append(mx)
        idxs.append(ix)
        work = jnp.where(lane == ix, -jnp.inf, work)
    ex = [jnp.exp(v - vals[0]) for v in vals]
    den = ex[0] + ex[1] + ex[2] + ex[3]

    onehot = jnp.zeros((tm, N_EXPERTS), F32)
    for ix in idxs:
        onehot = onehot + (lane == ix).astype(F32)
    ri = lax.broadcasted_iota(jnp.int32, (tm, tm), 0)
    ci = lax.broadcasted_iota(jnp.int32, (tm, tm), 1)
    tri = (ci < ri).astype(BF16)
    before = jnp.dot(tri, onehot.astype(BF16), preferred_element_type=F32) + carry_ref[...]

    col4 = lax.broadcasted_iota(jnp.int32, (tm, TOP_K), 1)
    e_out = jnp.zeros((tm, TOP_K), jnp.int32)
    g_out = jnp.zeros((tm, TOP_K), F32)
    r_out = jnp.zeros((tm, TOP_K), jnp.int32)
    for k in range(TOP_K):
        rank = jnp.sum(jnp.where(lane == idxs[k], before, 0.0), axis=-1, keepdims=True)
        e_out = jnp.where(col4 == k, idxs[k].astype(jnp.int32), e_out)
        g_out = jnp.where(col4 == k, ex[k] / den, g_out)
        r_out = jnp.where(col4 == k, rank.astype(jnp.int32), r_out)
    e_ref[...] = e_out
    g_ref[...] = g_out
    r_ref[...] = r_out
    carry_ref[...] = carry_ref[...] + jnp.sum(onehot, axis=0, keepdims=True)
    cnt_ref[...] = carry_ref[...]


MAX_COPIES_PER_STEP = 2048


def _copies_per_step(n):
    return max(c for c in range(8, MAX_COPIES_PER_STEP + 1, 8) if n % c == 0)


def _token_copy(src_ref, dst_ref, sem, src_tok, dst_tok):
    return pltpu.make_async_copy(src_ref.at[src_tok], dst_ref.at[dst_tok], sem)


def _drain_token_copies(src_ref, dst_ref, sem, n):
    def drain(i, carry):
        _token_copy(src_ref, dst_ref, sem, 0, 0).wait()
        return carry

    lax.fori_loop(0, n, drain, 0, unroll=8)


def _scatter_tokens_kernel(idx_ref, src_ref, out_ref, sem, *, n):
    def issue(h, carry):
        for prio in range(2):
            i = 2 * h + prio
            tok = lax.shift_right_logical(i, TOP_K.bit_length() - 1)
            _token_copy(src_ref, out_ref, sem, tok, idx_ref[0, i]).start(priority=prio)
        return carry

    lax.fori_loop(0, n // 2, issue, 0, unroll=4)
    _drain_token_copies(src_ref, out_ref, sem, n)


def scatter_tokens(src_tiles, idx, n_out):
    t = src_tiles.shape[0] // TILE_ROWS
    n = _copies_per_step(idx.shape[0])
    steps = idx.shape[0] // n
    out = pl.pallas_call(
        functools.partial(_scatter_tokens_kernel, n=n),
        out_shape=jax.ShapeDtypeStruct((n_out, TILE_ROWS, 128), F32),
        grid=(steps,),
        in_specs=[pl.BlockSpec((None, 1, n), lambda s: (s, 0, 0), memory_space=pltpu.SMEM),
                  pl.BlockSpec((n // TOP_K, TILE_ROWS, 128), lambda s: (s, 0, 0))],
        out_specs=pl.BlockSpec(memory_space=pl.ANY),
        scratch_shapes=[pltpu.SemaphoreType.DMA(())],
        compiler_params=_cparams(("arbitrary",)),
        name="scatter_tokens",
    )(idx.reshape(steps, 1, n), src_tiles.reshape(t, TILE_ROWS, 128))
    return out.reshape(n_out * TILE_ROWS, 128)


def _experts_kernel(be_ref, nx_ref, bv_ref, x_ref, wgu_hbm, bgu_ref, wd_hbm, bd_ref, o_ref,
                    wgu_f32, wd_f32, wgu_bf, wd_bf, sems, *, layer):
    s = pl.program_id(0)
    e = be_ref[s]
    valid = bv_ref[s]
    new_expert = jnp.logical_or(s == 0, e != be_ref[jnp.maximum(s - 1, 0)])
    used = valid > 0

    def weight_copies(expert):
        return (pltpu.make_async_copy(wgu_hbm.at[layer, expert], wgu_f32, sems.at[0]),
                pltpu.make_async_copy(wd_hbm.at[layer, expert], wd_f32, sems.at[1]))

    @pl.when(s == 0)
    def _():
        for cp in weight_copies(e):
            cp.start()

    @pl.when(jnp.logical_and(new_expert, used))
    def _():
        for cp in weight_copies(e):
            cp.wait()
        wgu_bf[...] = wgu_f32[...].astype(BF16)
        wd_bf[...] = wd_f32[...].astype(BF16)

        @pl.when(nx_ref[s] >= 0)
        def _():
            for cp in weight_copies(nx_ref[s]):
                cp.start()

    @pl.when(used)
    def _():
        rows = lax.broadcasted_iota(jnp.int32, (MOE_TM, 1), 0)
        x = jnp.where(rows < valid, _load_token_tiles(x_ref), 0.0).astype(BF16)
        h = jnp.dot(x, wgu_bf[...], preferred_element_type=F32) + bgu_ref[...]
        gate = jnp.minimum(h[:, :D_FF], SWIGLU_LIMIT)
        up = jnp.clip(h[:, D_FF:], -SWIGLU_LIMIT, SWIGLU_LIMIT)
        act = (up + 1.0) * (gate * jax.nn.sigmoid(SWIGLU_ALPHA * gate))
        y = jnp.dot(act.astype(BF16), wd_bf[...], preferred_element_type=F32) + bd_ref[...]
        _store_token_tiles(o_ref, y)

    @pl.when(jnp.logical_not(used))
    def _():
        o_ref[...] = jnp.zeros(o_ref.shape, o_ref.dtype)


def experts(xs, block_e, next_e, block_valid, w_gu, b_gu, w_down, b_down, layer):
    d = D_MODEL
    n_blocks = xs.shape[0] // (MOE_TM * TILE_ROWS)
    tile_block = pl.BlockSpec((MOE_TM * TILE_ROWS, 128), lambda s, be, nx, bv: (s, 0))
    any_spec = pl.BlockSpec(memory_space=pl.ANY)
    return pl.pallas_call(
        functools.partial(_experts_kernel, layer=layer),
        out_shape=jax.ShapeDtypeStruct(xs.shape, F32),
        grid_spec=pltpu.PrefetchScalarGridSpec(
            num_scalar_prefetch=3,
            grid=(n_blocks,),
            in_specs=[
                tile_block,
                any_spec,
                pl.BlockSpec((None, None, 1, 2 * D_FF), lambda s, be, nx, bv: (layer, be[s], 0, 0)),
                any_spec,
                pl.BlockSpec((None, None, 1, d), lambda s, be, nx, bv: (layer, be[s], 0, 0)),
            ],
            out_specs=tile_block,
            scratch_shapes=[
                pltpu.VMEM((d, 2 * D_FF), F32),
                pltpu.VMEM((D_FF, d), F32),
                pltpu.VMEM((d, 2 * D_FF), BF16),
                pltpu.VMEM((D_FF, d), BF16),
                pltpu.SemaphoreType.DMA((2,)),
            ],
        ),
        compiler_params=_cparams(("arbitrary",)),
        name="experts",
    )(block_e, next_e, block_valid, xs, w_gu, b_gu.reshape(DEPTH, N_EXPERTS, 1, 2 * D_FF),
      w_down, b_down.reshape(DEPTH, N_EXPERTS, 1, d))


def _combine_ln_kernel(idx0_ref, idxn_ref, ys_ref, x_ref, gt_ref, g_ref, b_ref, o_ref, buf_ref, sems):
    s = pl.program_id(0)
    n_tiles = pl.num_programs(0)
    n = TOP_K * ROW_TILE

    def copy(slot, i, src_tok):
        dst = buf_ref.at[slot, pl.ds(pl.multiple_of(i * TILE_ROWS, TILE_ROWS), TILE_ROWS), :]
        return pltpu.make_async_copy(ys_ref.at[src_tok], dst, sems.at[slot])

    def fetch(slot, idx_ref):
        def issue(h, carry):
            for prio in range(2):
                i = 2 * h + prio
                copy(slot, i, idx_ref[0, i]).start(priority=prio)
            return carry

        lax.fori_loop(0, n // 2, issue, 0, unroll=4)

    def drain(slot):
        def wait(i, carry):
            copy(slot, 0, 0).wait()
            return carry

        lax.fori_loop(0, n, wait, 0, unroll=8)

    @pl.when(s == 0)
    def _():
        fetch(0, idx0_ref)

    for slot in range(2):
        @pl.when(s % 2 == slot)
        def _():
            @pl.when(s + 1 < n_tiles)
            def _():
                fetch(1 - slot, idxn_ref)

            drain(slot)
            gt = gt_ref[...]
            f = None
            for k in range(TOP_K):
                base = k * ROW_TILE * TILE_ROWS
                y = jnp.concatenate(
                    [buf_ref[slot, pl.ds(base + c, ROW_TILE, stride=TILE_ROWS), :] for c in range(TILE_ROWS)],
                    axis=-1) * gt[:, k:k + 1]
                f = y if f is None else f + y
            o_ref[...] = _layer_norm_rows(ALPHA * x_ref[...] + f, g_ref[...], b_ref[...])


def combine_ln(x, ys, dest, gates, g_all, b_all, layer):
    t, d = x.shape
    nt = t // ROW_TILE
    n = TOP_K * ROW_TILE
    row = lambda i: (i, 0)
    idx = dest.reshape(nt, ROW_TILE, TOP_K).transpose(0, 2, 1).reshape(nt, 1, n)
    return pl.pallas_call(
        _combine_ln_kernel,
        out_shape=jax.ShapeDtypeStruct((t, d), F32),
        grid=(nt,),
        in_specs=[
            pl.BlockSpec((None, 1, n), lambda i: (0, 0, 0), memory_space=pltpu.SMEM),
            pl.BlockSpec((None, 1, n), lambda i: (jnp.minimum(i + 1, nt - 1), 0, 0),
                         memory_space=pltpu.SMEM),
            pl.BlockSpec(memory_space=pl.ANY),
            pl.BlockSpec((ROW_TILE, d), row),
            pl.BlockSpec((ROW_TILE, TOP_K), row),
            pl.BlockSpec((None, 1, d), lambda i: (layer, 0, 0)),
            pl.BlockSpec((None, 1, d), lambda i: (layer, 0, 0)),
        ],
        out_specs=pl.BlockSpec((ROW_TILE, d), row),
        scratch_shapes=[pltpu.VMEM((2, n * TILE_ROWS, 128), F32),
                        pltpu.SemaphoreType.DMA((2,))],
        compiler_params=_cparams(("arbitrary",)),
        name="combine_ln",
    )(idx, idx, ys.reshape(-1, TILE_ROWS, 128), x, gates, g_all.reshape(-1, 1, d), b_all.reshape(-1, 1, d))


def moe_layer(x, x_tiles, top_e, gates, rank, counts, layer, w_gu, b_gu, w_down, b_down, ln_g, ln_b):
    t, d = x.shape
    counts = counts.reshape(N_EXPERTS).astype(jnp.int32)
    padded = (counts + MOE_TM - 1) // MOE_TM * MOE_TM
    pad_end = jnp.cumsum(padded)
    pad_start = pad_end - padded
    n_blocks = -(-t * TOP_K // MOE_TM) + N_EXPERTS
    block_start = jnp.arange(n_blocks, dtype=jnp.int32) * MOE_TM
    block_e = jnp.minimum(jnp.sum(pad_end[None, :] <= block_start[:, None], axis=1),
                          N_EXPERTS - 1).astype(jnp.int32)
    n_used = pad_end[-1] // MOE_TM
    block_valid = jnp.where(block_start < pad_end[-1],
                            jnp.clip(pad_start[block_e] + counts[block_e] - block_start, 0, MOE_TM),
                            0).astype(jnp.int32)
    after = pad_end[block_e] // MOE_TM
    next_e = jnp.where(after < n_used, block_e[jnp.minimum(after, n_blocks - 1)], -1).astype(jnp.int32)
    dest = pad_start[top_e] + rank
    xs = scatter_tokens(x_tiles, dest.reshape(-1), n_blocks * MOE_TM)
    ys = experts(xs, block_e, next_e, block_valid, w_gu, b_gu, w_down, b_down, layer)
    return combine_ln(x, ys, dest, gates, ln_g, ln_b, layer)


def kernel(x_prompt, x_sample, cache_k_diff, cache_v_diff, cache_k_swa, cache_v_swa, page_table,
           ln1_g, ln1_b, ln2_g, ln2_b, w_qkv_diff, diff_lam, diff_subln_g, w_o_diff,
           w_qkv_swa, b_qkv_swa, swa_sinks, w_o_swa, b_o_swa,
           w_router, b_router, w_gu, b_gu, w_down, b_down):
    n_batch, seq, d = x_prompt.shape
    n_seq = x_sample.shape[0]
    tp = n_batch * seq
    x = jnp.concatenate([x_prompt.reshape(tp, d), x_sample.reshape(n_seq, d)], axis=0)

    n_layers_diff, n_phys = cache_k_diff.shape[:2]
    kt_cache = jnp.transpose(cache_k_diff, (0, 1, 3, 4, 5, 2)).reshape(
        n_layers_diff, n_phys, DIFF_K_DIM, PAGE_SIZE)
    v_cache = cache_v_diff.reshape(n_layers_diff, n_phys, PAGE_SIZE * DIFF_KV_HEADS, 2 * DIFF_HEAD_DIM)
    win_buf = cache_k_swa.shape[2]
    kt_swa = jnp.transpose(cache_k_swa, (0, 1, 3, 4, 2)).reshape(-1, n_seq, SWA_KV_DIM, win_buf)
    vt_swa = jnp.transpose(cache_v_swa, (0, 1, 3, 4, 2)).reshape(-1, n_seq, SWA_KV_DIM, win_buf)

    zero_bias_qkv = jnp.zeros((1, w_qkv_diff.shape[-1]), F32)
    zero_bias_o = jnp.zeros((1, d), F32)

    kdp, vdp, kds, vds = [], [], [], []
    ksp, vsp, kss, vss = [], [], [], []
    for i in range(DEPTH):
        j = i // 2
        if i % 2 == 0:
            kq, vq = DIFF_Q_DIM, DIFF_Q_DIM + DIFF_K_DIM
            h, (k_new, v_new) = linear(x, w_qkv_diff, j, zero_bias_qkv,
                                       col_ranges=((kq, vq), (vq, vq + DIFF_V_DIM)))
            hs = h[tp:]
            a_p = diff_prompt_attention(h, diff_lam, diff_subln_g, j, i, n_batch, seq)
            a_s = diff_decode_attention(hs, kt_cache, v_cache, page_table, diff_lam, diff_subln_g, j, i)
            kdp.append(k_new[:tp].reshape(n_batch, seq, DIFF_KV_HEADS, 2, DIFF_HEAD_DIM))
            vdp.append(v_new[:tp].reshape(n_batch, seq, DIFF_KV_HEADS, 2 * DIFF_HEAD_DIM))
            kds.append(k_new[tp:].reshape(n_seq, 1, DIFF_KV_HEADS, 2, DIFF_HEAD_DIM))
            vds.append(v_new[tp:].reshape(n_seq, 1, DIFF_KV_HEADS, 2 * DIFF_HEAD_DIM))
            a = jnp.concatenate([a_p, a_s], axis=0)
            x, x_tiles, *routing = oproj_ln_route(a, w_o_diff, j, zero_bias_o, x, ln1_g, ln1_b,
                                                  w_router, b_router, i)
        else:
            h, _ = linear(x, w_qkv_swa, j, b_qkv_swa[j].reshape(1, -1))
            hs = h[tp:]
            a_p = swa_prompt_attention(h, swa_sinks[j], n_batch, seq)
            a_s = swa_decode_attention(hs, kt_swa[j], vt_swa[j], swa_sinks[j])
            kq, vq = SWA_Q_DIM, SWA_Q_DIM + SWA_KV_DIM
            keep = min(WINDOW, seq)
            hp = h[:tp].reshape(n_batch, seq, -1)[:, seq - keep:]
            ksp.append(hp[..., kq:vq].reshape(n_batch, keep, SWA_KV_HEADS, SWA_HEAD_DIM))
            vsp.append(hp[..., vq:].reshape(n_batch, keep, SWA_KV_HEADS, SWA_HEAD_DIM))
            k_new = hs[:, kq:vq].reshape(n_seq, 1, SWA_KV_HEADS, SWA_HEAD_DIM)
            v_new = hs[:, vq:].reshape(n_seq, 1, SWA_KV_HEADS, SWA_HEAD_DIM)
            kss.append(jnp.concatenate([cache_k_swa[j], k_new], axis=1)[:, -win_buf:])
            vss.append(jnp.concatenate([cache_v_swa[j], v_new], axis=1)[:, -win_buf:])
            a = jnp.concatenate([a_p, a_s], axis=0)
            x, x_tiles, *routing = oproj_ln_route(a, w_o_swa, j, b_o_swa[j].reshape(1, -1), x, ln1_g, ln1_b,
                                                  w_router, b_router, i)
        x = moe_layer(x, x_tiles, *routing, i, w_gu, b_gu, w_down, b_down, ln2_g, ln2_b)

    return (x[:tp].reshape(n_batch, seq, d), x[tp:].reshape(n_seq, 1, d),
            jnp.stack(kdp), jnp.stack(vdp), jnp.stack(kds), jnp.stack(vds),
            jnp.stack(ksp), jnp.stack(vsp), jnp.stack(kss), jnp.stack(vss))
```

```python
import functools
import math

import jax
import jax.numpy as jnp
import numpy as np
from jax import lax
from jax.experimental import pallas as pl
from jax.experimental.pallas import tpu as pltpu

F32 = jnp.float32
BF16 = jnp.bfloat16

D_MODEL = 1024
DEPTH = 4
PAGE_SIZE = 128
DIFF_HEAD_DIM = 64
DIFF_HEADS = 8
DIFF_KV_HEADS = 2
DIFF_GROUP = 4
DIFF_Q_DIM = 1024
DIFF_K_DIM = 256
DIFF_V_DIM = 256
SWA_HEAD_DIM = 64
SWA_HEADS = 16
SWA_KV_HEADS = 2
SWA_GROUP = 8
SWA_Q_DIM = 1024
SWA_KV_DIM = 128
WINDOW = 128
N_EXPERTS = 32
TOP_K = 4
D_FF = 1024
SWIGLU_ALPHA = 1.702
SWIGLU_LIMIT = 7.0
ALPHA = (2 * DEPTH) ** 0.25
LN_EPS = 1e-5
NEG_INF = -1e30

ROW_TILE = 384
DIFF_TQ = 256
MOE_TM = 512
DECODE_PAGES = 16
SWA_SEQS = 8
VMEM_LIMIT = 56 * 1024 * 1024


def _lambda_init(layer):
    return 0.8 - 0.6 * math.exp(-0.3 * layer)


def _cparams(sem):
    return pltpu.CompilerParams(dimension_semantics=sem, vmem_limit_bytes=VMEM_LIMIT)


def _linear_kernel(x_ref, w_ref, b_ref, o_ref, *rest, col_ranges):
    col_refs, wbf_ref = rest[:-1], rest[-1]

    @pl.when(pl.program_id(0) == 0)
    def _():
        wbf_ref[...] = w_ref[...].astype(BF16)

    acc = jnp.dot(x_ref[...].astype(BF16), wbf_ref[...], preferred_element_type=F32) + b_ref[...]
    o_ref[...] = acc
    for (lo, hi), ref in zip(col_ranges, col_refs):
        ref[...] = acc[:, lo:hi]


def linear(x, w_all, layer, bias, col_ranges=()):
    t, k = x.shape
    n = w_all.shape[-1]
    row = lambda i: (i, 0)
    outs = pl.pallas_call(
        functools.partial(_linear_kernel, col_ranges=tuple(col_ranges)),
        out_shape=(jax.ShapeDtypeStruct((t, n), F32),
                   *[jax.ShapeDtypeStruct((t, hi - lo), F32) for lo, hi in col_ranges]),
        grid=(t // ROW_TILE,),
        in_specs=[
            pl.BlockSpec((ROW_TILE, k), row),
            pl.BlockSpec((None, k, n), lambda i: (layer, 0, 0)),
            pl.BlockSpec((1, n), lambda i: (0, 0)),
        ],
        out_specs=(pl.BlockSpec((ROW_TILE, n), row),
                   *[pl.BlockSpec((ROW_TILE, hi - lo), row) for lo, hi in col_ranges]),
        scratch_shapes=[pltpu.VMEM((k, n), BF16)],
        compiler_params=_cparams(("arbitrary",)),
        name="linear",
    )(x, w_all, bias)
    return outs[0], outs[1:]


def _layer_norm_rows(z, g, b):
    mu = jnp.mean(z, axis=-1, keepdims=True)
    zc = z - mu
    var = jnp.mean(zc * zc, axis=-1, keepdims=True)
    return zc * lax.rsqrt(var + LN_EPS) * g + b


TILE_ROWS = D_MODEL // 128


def _store_token_tiles(ref, val):
    n = val.shape[0]
    for c in range(TILE_ROWS):
        ref[pl.ds(c, n, stride=TILE_ROWS), :] = val[:, c * 128:(c + 1) * 128]


def _load_token_tiles(ref):
    n = ref.shape[0] // TILE_ROWS
    return jnp.concatenate([ref[pl.ds(c, n, stride=TILE_ROWS), :] for c in range(TILE_ROWS)], axis=-1)


def _oproj_ln_kernel(a_ref, w_ref, bo_ref, x_ref, g_ref, b_ref, wr_ref, br_ref,
                     o_ref, ot_ref, e_ref, gt_ref, r_ref, cnt_ref, wbf_ref, carry_ref):
    @pl.when(pl.program_id(0) == 0)
    def _():
        wbf_ref[...] = w_ref[...].astype(BF16)

    y = jnp.dot(a_ref[...], wbf_ref[...], preferred_element_type=F32) + bo_ref[...]
    z = _layer_norm_rows(ALPHA * x_ref[...] + y, g_ref[...], b_ref[...])
    o_ref[...] = z
    _store_token_tiles(ot_ref, z)
    _route_rows(z, wr_ref, br_ref, e_ref, gt_ref, r_ref, cnt_ref, carry_ref)


def oproj_ln_route(a, w_all, layer_slot, b_o, x, g_all, b_all, w_router, b_router, layer):
    t, d = x.shape
    row = lambda i: (i, 0)
    return pl.pallas_call(
        _oproj_ln_kernel,
        out_shape=(jax.ShapeDtypeStruct((t, d), F32),
                   jax.ShapeDtypeStruct((t * TILE_ROWS, 128), F32),
                   jax.ShapeDtypeStruct((t, TOP_K), jnp.int32),
                   jax.ShapeDtypeStruct((t, TOP_K), F32),
                   jax.ShapeDtypeStruct((t, TOP_K), jnp.int32),
                   jax.ShapeDtypeStruct((1, N_EXPERTS), F32)),
        grid=(t // ROW_TILE,),
        in_specs=[
            pl.BlockSpec((ROW_TILE, d), row),
            pl.BlockSpec((None, d, d), lambda i: (layer_slot, 0, 0)),
            pl.BlockSpec((1, d), lambda i: (0, 0)),
            pl.BlockSpec((ROW_TILE, d), row),
            pl.BlockSpec((None, 1, d), lambda i: (layer, 0, 0)),
            pl.BlockSpec((None, 1, d), lambda i: (layer, 0, 0)),
            pl.BlockSpec((None, d, N_EXPERTS), lambda i: (layer, 0, 0)),
            pl.BlockSpec((None, 1, N_EXPERTS), lambda i: (layer, 0, 0)),
        ],
        out_specs=(pl.BlockSpec((ROW_TILE, d), row),
                   pl.BlockSpec((ROW_TILE * TILE_ROWS, 128), row),
                   pl.BlockSpec((ROW_TILE, TOP_K), row),
                   pl.BlockSpec((ROW_TILE, TOP_K), row),
                   pl.BlockSpec((ROW_TILE, TOP_K), row),
                   pl.BlockSpec((1, N_EXPERTS), lambda i: (0, 0))),
        scratch_shapes=[pltpu.VMEM((d, d), BF16), pltpu.VMEM((1, N_EXPERTS), F32)],
        compiler_params=_cparams(("arbitrary",)),
        name="oproj_ln_route",
    )(a, w_all, b_o, x, g_all.reshape(-1, 1, d), b_all.reshape(-1, 1, d),
      w_router, b_router.reshape(-1, 1, N_EXPERTS))


def _diff_prompt_kernel(lam_ref, sg_ref, q_ref, k_ref, v_ref, o_ref,
                        qt_ref, m_ref, l_ref, acc_ref, *, layer):
    tq = DIFF_TQ
    dh = DIFF_HEAD_DIM
    dv = 2 * dh
    kvh = pl.program_id(1)
    qi = pl.program_id(2)

    qt = (q_ref[...] * (dh ** -0.5)).T
    arow = lax.broadcasted_iota(jnp.int32, (dh, tq), 0)
    q_off = lax.broadcasted_iota(jnp.int32, (dh, tq), 1).astype(F32)
    for g in range(DIFF_GROUP):
        slope = jnp.where(kvh == 0, 2.0 ** -(g + 1), 2.0 ** -(DIFF_GROUP + g + 1)).astype(F32)
        extra = jnp.where(arow < 2, slope, jnp.where(arow == 2, -slope * q_off, 0.0))
        q0 = qt[(g * 2) * dh:(g * 2 + 1) * dh]
        q1 = qt[(g * 2 + 1) * dh:(g * 2 + 2) * dh]
        qt_ref[0, :, g * tq:(g + 1) * tq] = jnp.concatenate([q0, extra], axis=0).astype(BF16)
        qt_ref[1, :, g * tq:(g + 1) * tq] = jnp.concatenate([extra, q1], axis=0).astype(BF16)
    m_ref[...] = jnp.full(m_ref.shape, NEG_INF, F32)
    l_ref[...] = jnp.zeros(l_ref.shape, F32)
    acc_ref[...] = jnp.zeros(acc_ref.shape, F32)

    krow = lax.broadcasted_iota(jnp.int32, (tq, tq), 0)
    qcol = lax.broadcasted_iota(jnp.int32, (tq, tq), 1)
    causal = krow <= qcol

    def block(j, nk, masked):
        start = pl.multiple_of(j * tq, tq)
        kb = k_ref[pl.ds(start, nk), :]
        vt = v_ref[pl.ds(start, nk), :].T.astype(BF16)
        lane = lax.broadcasted_iota(jnp.int32, (nk, dv), 1)
        row = lax.broadcasted_iota(jnp.int32, (nk, dv), 0)
        k_off = jnp.bitwise_and(row, tq - 1).astype(F32)
        shift = ((j - qi) * tq).astype(F32) + jnp.bitwise_and(row, -tq).astype(F32)
        keys = [
            jnp.where(lane < dh, kb,
                      jnp.where(lane == dh, k_off,
                                jnp.where(lane == dh + 1, shift, jnp.where(lane == dh + 2, 1.0, 0.0)))),
            jnp.where(lane >= dh, kb,
                      jnp.where(lane == 0, k_off,
                                jnp.where(lane == 1, shift, jnp.where(lane == 2, 1.0, 0.0)))),
        ]
        for c in range(2):
            s_all = jnp.dot(keys[c].astype(BF16), qt_ref[c], preferred_element_type=F32)
            for g in range(DIFF_GROUP):
                idx = g * 2 + c
                s = s_all[:, g * tq:(g + 1) * tq]
                if masked:
                    s = jnp.where(causal, s, NEG_INF)
                m_old = m_ref[idx:idx + 1, :]
                m_new = jnp.maximum(m_old, jnp.max(s, axis=0, keepdims=True))
                alpha = jnp.exp(m_old - m_new)
                p = jnp.exp(s - m_new)
                l_ref[idx:idx + 1, :] = alpha * l_ref[idx:idx + 1, :] + jnp.sum(p, axis=0, keepdims=True)
                acc_ref[idx] = alpha * acc_ref[idx] + jnp.dot(vt, p.astype(BF16),
                                                              preferred_element_type=F32)
                m_ref[idx:idx + 1, :] = m_new

    def pair_body(p, carry):
        block(2 * p, 2 * tq, False)
        return carry

    lax.fori_loop(0, qi // 2, pair_body, 0)

    @pl.when(qi % 2 == 1)
    def _():
        block(qi - 1, tq, False)

    block(qi, tq, True)

    lp = lam_ref[...]
    lam = (jnp.exp(jnp.sum(lp[0:1] * lp[1:2], axis=-1, keepdims=True))
           - jnp.exp(jnp.sum(lp[2:3] * lp[3:4], axis=-1, keepdims=True)) + _lambda_init(layer))
    gain = sg_ref[...] * (1.0 - _lambda_init(layer))
    for g in range(DIFF_GROUP):
        o1 = acc_ref[g * 2] * (1.0 / l_ref[g * 2:g * 2 + 1, :])
        o2 = acc_ref[g * 2 + 1] * (1.0 / l_ref[g * 2 + 1:g * 2 + 2, :])
        ot = o1 - lam * o2
        ot = ot * lax.rsqrt(jnp.mean(ot * ot, axis=0, keepdims=True) + LN_EPS)
        o_ref[:, g * dv:(g + 1) * dv] = (ot.T * gain).astype(o_ref.dtype)


def diff_prompt_attention(h, diff_lam, subln_g, slot, layer, n_batch, seq):
    tq = DIFF_TQ
    nq = seq // tq
    qw = DIFF_GROUP * 2 * DIFF_HEAD_DIM
    kw = 2 * DIFF_HEAD_DIM
    return pl.pallas_call(
        functools.partial(_diff_prompt_kernel, layer=layer),
        out_shape=jax.ShapeDtypeStruct((n_batch * seq, DIFF_Q_DIM), BF16),
        grid=(n_batch, DIFF_KV_HEADS, nq),
        in_specs=[
            pl.BlockSpec((None, 4, DIFF_HEAD_DIM), lambda b, h_, i: (slot, 0, 0)),
            pl.BlockSpec((None, 1, kw), lambda b, h_, i: (slot, 0, 0)),
            pl.BlockSpec((tq, qw), lambda b, h_, i: (b * nq + i, h_)),
            pl.BlockSpec((seq, kw), lambda b, h_, i: (b, DIFF_Q_DIM // kw + h_)),
            pl.BlockSpec((seq, kw), lambda b, h_, i: (b, (DIFF_Q_DIM + DIFF_K_DIM) // kw + h_)),
        ],
        out_specs=pl.BlockSpec((tq, qw), lambda b, h_, i: (b * nq + i, h_)),
        scratch_shapes=[
            pltpu.VMEM((2, 2 * DIFF_HEAD_DIM, DIFF_GROUP * tq), BF16),
            pltpu.VMEM((2 * DIFF_GROUP, tq), F32),
            pltpu.VMEM((2 * DIFF_GROUP, tq), F32),
            pltpu.VMEM((2 * DIFF_GROUP, 2 * DIFF_HEAD_DIM, tq), F32),
        ],
        compiler_params=_cparams(("arbitrary", "arbitrary", "arbitrary")),
        name="diff_prompt_attn",
    )(diff_lam, subln_g.reshape(-1, 1, kw), h, h, h)


def _diff_decode_kernel(pt_ref, lam_ref, sg_ref, slope_ref, wq_ref, kn_ref, vn_ref, *rest,
                        layer, n_pages, past_len):
    del pt_ref
    np_ = DECODE_PAGES
    k_refs = rest[:np_]
    v_refs = rest[np_:2 * np_]
    o_ref = rest[2 * np_]
    m_ref, l_ref, acc_ref = rest[2 * np_ + 1:]
    jc = pl.program_id(1)
    n_chunks = n_pages // np_
    dv = 2 * DIFF_HEAD_DIM

    @pl.when(jc == 0)
    def _():
        m_ref[...] = jnp.full(m_ref.shape, NEG_INF, F32)
        l_ref[...] = jnp.zeros(l_ref.shape, F32)
        acc_ref[...] = jnp.zeros(acc_ref.shape, F32)

    wq = wq_ref[...]
    slope = slope_ref[...]
    lane = lax.broadcasted_iota(jnp.int32, (1, PAGE_SIZE), 1)
    rowi = lax.broadcasted_iota(jnp.int32, (16, 1), 0)
    is_kv0 = (rowi % (2 * DIFF_GROUP)) < DIFF_GROUP

    s_pages = []
    for g in range(np_):
        kt = k_refs[g][...].astype(BF16)
        s = jnp.dot(wq, kt, preferred_element_type=F32)
        k_pos = (jc * np_ + g) * PAGE_SIZE + lane
        s_pages.append(s - slope * (past_len - k_pos).astype(F32))
    m_old = m_ref[...]
    m_new = m_old
    for s in s_pages:
        m_new = jnp.maximum(m_new, jnp.max(s, axis=-1, keepdims=True))
    alpha = jnp.exp(m_old - m_new)
    l_new = alpha * l_ref[...]
    acc = alpha * acc_ref[...]
    for g in range(np_):
        p = jnp.exp(s_pages[g] - m_new)
        l_new = l_new + jnp.sum(p, axis=-1, keepdims=True)
        pb = p.astype(BF16)
        v0 = v_refs[g][pl.ds(0, PAGE_SIZE, stride=DIFF_KV_HEADS), :].astype(BF16)
        v1 = v_refs[g][pl.ds(1, PAGE_SIZE, stride=DIFF_KV_HEADS), :].astype(BF16)
        acc = acc + jnp.where(is_kv0, jnp.dot(pb, v0, preferred_element_type=F32),
                              jnp.dot(pb, v1, preferred_element_type=F32))
    m_ref[...] = m_new
    l_ref[...] = l_new
    acc_ref[...] = acc

    @pl.when(jc == n_chunks - 1)
    def _():
        kn = kn_ref[...]
        vn = vn_ref[...]
        s_n = jnp.sum(wq.astype(F32) * kn, axis=-1, keepdims=True)
        m_o = m_ref[...]
        m_f = jnp.maximum(m_o, s_n)
        a_f = jnp.exp(m_o - m_f)
        p_n = jnp.exp(s_n - m_f)
        l_f = a_f * l_ref[...] + p_n
        v_rows = jnp.where(is_kv0, vn[:, :dv], vn[:, dv:])
        acc_f = a_f * acc_ref[...] + p_n * v_rows
        o_maps = acc_f / l_f
        lp = lam_ref[...]
        lam = (jnp.exp(jnp.sum(lp[0:1] * lp[1:2], axis=-1, keepdims=True))
               - jnp.exp(jnp.sum(lp[2:3] * lp[3:4], axis=-1, keepdims=True)) + _lambda_init(layer))
        o = o_maps[:DIFF_HEADS] - lam * o_maps[DIFF_HEADS:]
        gain = sg_ref[...] * (1.0 - _lambda_init(layer))
        o = o * lax.rsqrt(jnp.mean(o * o, axis=-1, keepdims=True) + LN_EPS) * gain
        o_ref[...] = o.astype(o_ref.dtype)


def diff_decode_attention(hs, kt_cache, v_cache, page_table, diff_lam, subln_g, slot, layer):
    n_seq = hs.shape[0]
    n_pages = page_table.shape[1]
    past_len = n_pages * PAGE_SIZE
    dh = DIFF_HEAD_DIM
    q = hs[:, :DIFF_Q_DIM].reshape(n_seq, DIFF_KV_HEADS, DIFF_GROUP, 2, dh) * (dh ** -0.5)
    eye = jnp.eye(2, dtype=F32)
    wq = jnp.einsum('bkgcd,kK,cC->bckgKCd', q, eye, eye).reshape(n_seq, 16, DIFF_K_DIM).astype(BF16)
    kn = hs[:, DIFF_Q_DIM:DIFF_Q_DIM + DIFF_K_DIM].reshape(n_seq, 1, DIFF_K_DIM)
    vn = hs[:, DIFF_Q_DIM + DIFF_K_DIM:].reshape(n_seq, 1, DIFF_V_DIM)
    heads = np.arange(DIFF_HEADS, dtype=np.float32).reshape(DIFF_KV_HEADS, DIFF_GROUP)
    slopes = np.tile((2.0 ** -(heads + 1.0)).reshape(1, DIFF_HEADS), (2, 1)).reshape(16, 1)
    slopes = jnp.asarray(slopes, F32)

    np_ = DECODE_PAGES
    n_chunks = n_pages // np_

    def page_map(g):
        return lambda b, jc, pt: (slot, pt[b, jc * np_ + g], 0, 0)

    in_specs = [
        pl.BlockSpec((None, 4, dh), lambda b, jc, pt: (slot, 0, 0)),
        pl.BlockSpec((None, 1, 2 * dh), lambda b, jc, pt: (slot, 0, 0)),
        pl.BlockSpec((16, 1), lambda b, jc, pt: (0, 0)),
        pl.BlockSpec((None, 16, DIFF_K_DIM), lambda b, jc, pt: (b, 0, 0)),
        pl.BlockSpec((None, 1, DIFF_K_DIM), lambda b, jc, pt: (b, 0, 0)),
        pl.BlockSpec((None, 1, DIFF_V_DIM), lambda b, jc, pt: (b, 0, 0)),
    ]
    in_specs += [pl.BlockSpec((None, None, DIFF_K_DIM, PAGE_SIZE), page_map(g)) for g in range(np_)]
    in_specs += [pl.BlockSpec((None, None, PAGE_SIZE * DIFF_KV_HEADS, 2 * dh), page_map(g))
                 for g in range(np_)]
    out = pl.pallas_call(
        functools.partial(_diff_decode_kernel, layer=layer, n_pages=n_pages, past_len=past_len),
        out_shape=jax.ShapeDtypeStruct((n_seq, DIFF_HEADS, 2 * dh), BF16),
        grid_spec=pltpu.PrefetchScalarGridSpec(
            num_scalar_prefetch=1,
            grid=(n_seq, n_chunks),
            in_specs=in_specs,
            out_specs=pl.BlockSpec((None, DIFF_HEADS, 2 * dh), lambda b, jc, pt: (b, 0, 0)),
            scratch_shapes=[
                pltpu.VMEM((16, 1), F32),
                pltpu.VMEM((16, 1), F32),
                pltpu.VMEM((16, 2 * dh), F32),
            ],
        ),
        compiler_params=_cparams(("arbitrary", "arbitrary")),
        name="diff_decode_attn",
    )(page_table, diff_lam, subln_g.reshape(-1, 1, 2 * dh), slopes, wq, kn, vn,
      *([kt_cache] * np_), *([v_cache] * np_))
    return out.reshape(n_seq, DIFF_Q_DIM)


def _swa_slopes():
    return 2.0 ** (-8.0 * np.arange(1, SWA_HEADS + 1, dtype=np.float32) / SWA_HEADS)


def _swa_prompt_kernel(sink_ref, slope_ref, q_ref, cur_ref, prev_ref, o_ref, *, n_blocks):
    w = WINDOW
    dh = SWA_HEAD_DIM
    bi = pl.program_id(0) % n_blocks
    krow = lax.broadcasted_iota(jnp.int32, (2 * w, w), 0)
    qcol = lax.broadcasted_iota(jnp.int32, (2 * w, w), 1)
    dist = (qcol - krow + w).astype(F32)
    in_seq = jnp.logical_or(krow >= w, (jnp.zeros_like(krow) + bi) > 0)
    ok = jnp.logical_and(jnp.logical_and(krow > qcol, krow <= qcol + w), in_seq)

    qt = (q_ref[...] * (dh ** -0.5)).T.astype(BF16)
    kk = jnp.concatenate([prev_ref[:, :SWA_KV_DIM], cur_ref[:, :SWA_KV_DIM]], axis=0).astype(BF16)
    vt = jnp.concatenate([prev_ref[:, SWA_KV_DIM:], cur_ref[:, SWA_KV_DIM:]], axis=0).T.astype(BF16)
    for kh in range(SWA_KV_HEADS):
        q_heads = jnp.concatenate(
            [qt[(kh * SWA_GROUP + g) * dh:(kh * SWA_GROUP + g + 1) * dh] for g in range(SWA_GROUP)],
            axis=1)
        s_all = jnp.dot(kk[:, kh * dh:(kh + 1) * dh], q_heads, preferred_element_type=F32)
        probs = []
        for g in range(SWA_GROUP):
            hd = kh * SWA_GROUP + g
            sink = sink_ref[hd]
            s = jnp.where(ok, s_all[:, g * w:(g + 1) * w] - slope_ref[hd] * dist, NEG_INF)
            m = jnp.maximum(jnp.max(s, axis=0, keepdims=True), sink)
            e = jnp.exp(s - m)
            den = jnp.sum(e, axis=0, keepdims=True) + jnp.exp(sink - m)
            probs.append((e * (1.0 / den)).astype(BF16))
        ot = jnp.dot(vt[kh * dh:(kh + 1) * dh], jnp.concatenate(probs, axis=1),
                     preferred_element_type=F32)
        for g in range(0, SWA_GROUP, 2):
            pair = jnp.concatenate([ot[:, g * w:(g + 1) * w], ot[:, (g + 1) * w:(g + 2) * w]], axis=0)
            hd = kh * SWA_GROUP + g
            o_ref[:, hd * dh:(hd + 2) * dh] = pair.T.astype(o_ref.dtype)


def swa_prompt_attention(h, sinks, n_batch, seq):
    nb = seq // WINDOW
    kvw = 2 * SWA_KV_DIM
    kv_col = SWA_Q_DIM // kvw
    slopes = jnp.asarray(_swa_slopes(), F32)
    smem = pl.BlockSpec(memory_space=pltpu.SMEM)
    return pl.pallas_call(
        functools.partial(_swa_prompt_kernel, n_blocks=nb),
        out_shape=jax.ShapeDtypeStruct((n_batch * seq, SWA_Q_DIM), BF16),
        grid=(n_batch * nb,),
        in_specs=[
            smem, smem,
            pl.BlockSpec((WINDOW, SWA_Q_DIM), lambda r: (r, 0)),
            pl.BlockSpec((WINDOW, kvw), lambda r: (r, kv_col)),
            pl.BlockSpec((WINDOW, kvw), lambda r: (jnp.maximum(r - 1, 0), kv_col)),
        ],
        out_specs=pl.BlockSpec((WINDOW, SWA_Q_DIM), lambda r: (r, 0)),
        compiler_params=_cparams(("arbitrary",)),
        name="swa_prompt_attn",
    )(sinks, slopes, h, h, h)


def _swa_decode_kernel(sink_ref, slope_ref, wq_ref, kn_ref, vn_ref, kt_ref, vt_ref, o_ref, *, win_buf):
    dh = SWA_HEAD_DIM
    lane = lax.broadcasted_iota(jnp.int32, (1, win_buf), 1)
    dist = (win_buf - lane).astype(F32)
    ok = dist < float(WINDOW)
    rowi = lax.broadcasted_iota(jnp.int32, (SWA_HEADS, 1), 0)
    is_kv0 = rowi < SWA_GROUP
    sink = sink_ref[...]
    slope = slope_ref[...]
    for sq in range(SWA_SEQS):
        wq = wq_ref[sq]
        kt = kt_ref[sq].astype(BF16)
        vt = vt_ref[sq].astype(BF16)
        s = jnp.dot(wq, kt, preferred_element_type=F32)
        s = jnp.where(ok, s - slope * dist, NEG_INF)
        s_n = jnp.sum(wq.astype(F32) * kn_ref[sq], axis=-1, keepdims=True)
        m = jnp.maximum(jnp.maximum(jnp.max(s, axis=-1, keepdims=True), s_n), sink)
        e = jnp.exp(s - m)
        e_n = jnp.exp(s_n - m)
        den = jnp.sum(e, axis=-1, keepdims=True) + e_n + jnp.exp(sink - m)
        inv = 1.0 / den
        pv = lax.dot_general((e * inv).astype(BF16), vt, (((1,), (1,)), ((), ())),
                             preferred_element_type=F32)
        pv = pv + (e_n * inv) * vn_ref[sq]
        o_ref[sq] = jnp.where(is_kv0, pv[:, :dh], pv[:, dh:]).astype(o_ref.dtype)


def swa_decode_attention(hs, kt_buf, vt_buf, sinks):
    n_seq = hs.shape[0]
    win_buf = kt_buf.shape[-1]
    dh = SWA_HEAD_DIM
    q = hs[:, :SWA_Q_DIM].reshape(n_seq, SWA_KV_HEADS, SWA_GROUP, dh) * (dh ** -0.5)
    eye = jnp.eye(SWA_KV_HEADS, dtype=F32)
    wq = jnp.einsum('bkgd,kK->bkgKd', q, eye).reshape(n_seq, SWA_HEADS, SWA_KV_DIM).astype(BF16)
    kn = hs[:, SWA_Q_DIM:SWA_Q_DIM + SWA_KV_DIM].reshape(n_seq, 1, SWA_KV_DIM)
    vn = hs[:, SWA_Q_DIM + SWA_KV_DIM:].reshape(n_seq, 1, SWA_KV_DIM)
    slopes = jnp.asarray(_swa_slopes().reshape(SWA_HEADS, 1), F32)
    sb = SWA_SEQS
    seq3 = lambda i: (i, 0, 0)
    out = pl.pallas_call(
        functools.partial(_swa_decode_kernel, win_buf=win_buf),
        out_shape=jax.ShapeDtypeStruct((n_seq, SWA_HEADS, dh), BF16),
        grid=(n_seq // sb,),
        in_specs=[
            pl.BlockSpec((SWA_HEADS, 1), lambda i: (0, 0)),
            pl.BlockSpec((SWA_HEADS, 1), lambda i: (0, 0)),
            pl.BlockSpec((sb, SWA_HEADS, SWA_KV_DIM), seq3),
            pl.BlockSpec((sb, 1, SWA_KV_DIM), seq3),
            pl.BlockSpec((sb, 1, SWA_KV_DIM), seq3),
            pl.BlockSpec((sb, SWA_KV_DIM, win_buf), seq3),
            pl.BlockSpec((sb, SWA_KV_DIM, win_buf), seq3),
        ],
        out_specs=pl.BlockSpec((sb, SWA_HEADS, dh), seq3),
        compiler_params=_cparams(("arbitrary",)),
        name="swa_decode_attn",
    )(sinks.reshape(SWA_HEADS, 1), slopes, wq, kn, vn, kt_buf, vt_buf)
    return out.reshape(n_seq, SWA_Q_DIM)


def _route_rows(x, w_ref, b_ref, e_ref, g_ref, r_ref, cnt_ref, carry_ref):
    tm = x.shape[0]

    @pl.when(pl.program_id(0) == 0)
    def _():
        carry_ref[...] = jnp.zeros(carry_ref.shape, F32)

    logits = jnp.dot(x.astype(BF16), w_ref[...].astype(BF16),
                     preferred_element_type=F32) + b_ref[...]
    lane = lax.broadcasted_iota(jnp.int32, (tm, N_EXPERTS), 1).astype(F32)
    work = logits
    vals, idxs = [], []
    for _ in range(TOP_K):
        mx = jnp.max(work, axis=-1, keepdims=True)
        ix = jnp.min(jnp.where(work == mx, lane, float(N_EXPERTS)), axis=-1, keepdims=True)
        vals.append(mx)
        idxs.append(ix)
        work = jnp.where(lane == ix, -jnp.inf, work)
    ex = [jnp.exp(v - vals[0]) for v in vals]
    den = ex[0] + ex[1] + ex[2] + ex[3]

    onehot = jnp.zeros((tm, N_EXPERTS), F32)
    for ix in idxs:
        onehot = onehot + (lane == ix).astype(F32)
    ri = lax.broadcasted_iota(jnp.int32, (tm, tm), 0)
    ci = lax.broadcasted_iota(jnp.int32, (tm, tm), 1)
    tri = (ci < ri).astype(BF16)
    before = jnp.dot(tri, onehot.astype(BF16), preferred_element_type=F32) + carry_ref[...]

    col4 = lax.broadcasted_iota(jnp.int32, (tm, TOP_K), 1)
    e_out = jnp.zeros((tm, TOP_K), jnp.int32)
    g_out = jnp.zeros((tm, TOP_K), F32)
    r_out = jnp.zeros((tm, TOP_K), jnp.int32)
    for k in range(TOP_K):
        rank = jnp.sum(jnp.where(lane == idxs[k], before, 0.0), axis=-1, keepdims=True)
        e_out = jnp.where(col4 == k, idxs[k].astype(jnp.int32), e_out)
        g_out = jnp.where(col4 == k, ex[k] / den, g_out)
        r_out = jnp.where(col4 == k, rank.astype(jnp.int32), r_out)
    e_ref[...] = e_out
    g_ref[...] = g_out
    r_ref[...] = r_out
    carry_ref[...] = carry_ref[...] + jnp.sum(onehot, axis=0, keepdims=True)
    cnt_ref[...] = carry_ref[...]


MAX_COPIES_PER_STEP = 2048


def _copies_per_step(n):
    return max(c for c in range(8, MAX_COPIES_PER_STEP + 1, 8) if n % c == 0)


def _token_copy(src_ref, dst_ref, sem, src_tok, dst_tok):
    return pltpu.make_async_copy(src_ref.at[src_tok], dst_ref.at[dst_tok], sem)


def _drain_token_copies(src_ref, dst_ref, sem, n):
    def drain(i, carry):
        _token_copy(src_ref, dst_ref, sem, 0, 0).wait()
        return carry

    lax.fori_loop(0, n, drain, 0, unroll=8)


def _scatter_tokens_kernel(idx_ref, src_ref, out_ref, sem, *, n):
    def issue(h, carry):
        for prio in range(2):
            i = 2 * h + prio
            tok = lax.shift_right_logical(i, TOP_K.bit_length() - 1)
            _token_copy(src_ref, out_ref, sem, tok, idx_ref[0, i]).start(priority=prio)
        return carry

    lax.fori_loop(0, n // 2, issue, 0, unroll=4)
    _drain_token_copies(src_ref, out_ref, sem, n)


def scatter_tokens(src_tiles, idx, n_out):
    t = src_tiles.shape[0] // TILE_ROWS
    n = _copies_per_step(idx.shape[0])
    steps = idx.shape[0] // n
    out = pl.pallas_call(
        functools.partial(_scatter_tokens_kernel, n=n),
        out_shape=jax.ShapeDtypeStruct((n_out, TILE_ROWS, 128), F32),
        grid=(steps,),
        in_specs=[pl.BlockSpec((None, 1, n), lambda s: (s, 0, 0), memory_space=pltpu.SMEM),
                  pl.BlockSpec((n // TOP_K, TILE_ROWS, 128), lambda s: (s, 0, 0))],
        out_specs=pl.BlockSpec(memory_space=pl.ANY),
        scratch_shapes=[pltpu.SemaphoreType.DMA(())],
        compiler_params=_cparams(("arbitrary",)),
        name="scatter_tokens",
    )(idx.reshape(steps, 1, n), src_tiles.reshape(t, TILE_ROWS, 128))
    return out.reshape(n_out * TILE_ROWS, 128)


def _experts_kernel(be_ref, nx_ref, bv_ref, x_ref, wgu_hbm, bgu_ref, wd_hbm, bd_ref, o_ref,
                    wgu_f32, wd_f32, wgu_bf, wd_bf, sems, *, layer):
    s = pl.program_id(0)
    e = be_ref[s]
    valid = bv_ref[s]
    new_expert = jnp.logical_or(s == 0, e != be_ref[jnp.maximum(s - 1, 0)])
    used = valid > 0

    def weight_copies(expert):
        return (pltpu.make_async_copy(wgu_hbm.at[layer, expert], wgu_f32, sems.at[0]),
                pltpu.make_async_copy(wd_hbm.at[layer, expert], wd_f32, sems.at[1]))

    @pl.when(s == 0)
    def _():
        for cp in weight_copies(e):
            cp.start()

    @pl.when(jnp.logical_and(new_expert, used))
    def _():
        for cp in weight_copies(e):
            cp.wait()
        wgu_bf[...] = wgu_f32[...].astype(BF16)
        wd_bf[...] = wd_f32[...].astype(BF16)

        @pl.when(nx_ref[s] >= 0)
        def _():
            for cp in weight_copies(nx_ref[s]):
                cp.start()

    quarter = MOE_TM // 4
    quarters = (valid + quarter - 1) // quarter
    for nq in range(1, 5):
        @pl.when(quarters == nq)
        def _():
            n = nq * quarter
            rows = lax.broadcasted_iota(jnp.int32, (n, 1), 0)
            tiles = jnp.concatenate(
                [x_ref[pl.ds(c, n, stride=TILE_ROWS), :] for c in range(TILE_ROWS)], axis=-1)
            x = jnp.where(rows < valid, tiles, 0.0).astype(BF16)
            h = jnp.dot(x, wgu_bf[...], preferred_element_type=F32) + bgu_ref[...]
            gate = jnp.minimum(h[:, :D_FF], SWIGLU_LIMIT)
            up = jnp.clip(h[:, D_FF:], -SWIGLU_LIMIT, SWIGLU_LIMIT)
            act = (up + 1.0) * (gate * jax.nn.sigmoid(SWIGLU_ALPHA * gate))
            y = jnp.dot(act.astype(BF16), wd_bf[...], preferred_element_type=F32) + bd_ref[...]
            for c in range(TILE_ROWS):
                o_ref[pl.ds(c, n, stride=TILE_ROWS), :] = y[:, c * 128:(c + 1) * 128]
            if n < MOE_TM:
                o_ref[n * TILE_ROWS:, :] = jnp.zeros(((MOE_TM - n) * TILE_ROWS, 128), o_ref.dtype)

    @pl.when(jnp.logical_not(used))
    def _():
        o_ref[...] = jnp.zeros(o_ref.shape, o_ref.dtype)


def experts(xs, block_e, next_e, block_valid, w_gu, b_gu, w_down, b_down, layer):
    d = D_MODEL
    n_blocks = xs.shape[0] // (MOE_TM * TILE_ROWS)
    tile_block = pl.BlockSpec((MOE_TM * TILE_ROWS, 128), lambda s, be, nx, bv: (s, 0))
    any_spec = pl.BlockSpec(memory_space=pl.ANY)
    return pl.pallas_call(
        functools.partial(_experts_kernel, layer=layer),
        out_shape=jax.ShapeDtypeStruct(xs.shape, F32),
        grid_spec=pltpu.PrefetchScalarGridSpec(
            num_scalar_prefetch=3,
            grid=(n_blocks,),
            in_specs=[
                tile_block,
                any_spec,
                pl.BlockSpec((None, None, 1, 2 * D_FF), lambda s, be, nx, bv: (layer, be[s], 0, 0)),
                any_spec,
                pl.BlockSpec((None, None, 1, d), lambda s, be, nx, bv: (layer, be[s], 0, 0)),
            ],
            out_specs=tile_block,
            scratch_shapes=[
                pltpu.VMEM((d, 2 * D_FF), F32),
                pltpu.VMEM((D_FF, d), F32),
                pltpu.VMEM((d, 2 * D_FF), BF16),
                pltpu.VMEM((D_FF, d), BF16),
                pltpu.SemaphoreType.DMA((2,)),
            ],
        ),
        compiler_params=_cparams(("arbitrary",)),
        name="experts",
    )(block_e, next_e, block_valid, xs, w_gu, b_gu.reshape(DEPTH, N_EXPERTS, 1, 2 * D_FF),
      w_down, b_down.reshape(DEPTH, N_EXPERTS, 1, d))


def _combine_ln_kernel(idx0_ref, idxn_ref, ys_ref, x_ref, gt_ref, g_ref, b_ref, o_ref, buf_ref, sems):
    s = pl.program_id(0)
    n_tiles = pl.num_programs(0)
    n = TOP_K * ROW_TILE

    def copy(slot, i, src_tok):
        dst = buf_ref.at[slot, pl.ds(pl.multiple_of(i * TILE_ROWS, TILE_ROWS), TILE_ROWS), :]
        return pltpu.make_async_copy(ys_ref.at[src_tok], dst, sems.at[slot])

    def fetch(slot, idx_ref):
        def issue(h, carry):
            for prio in range(2):
                i = 2 * h + prio
                copy(slot, i, idx_ref[0, i]).start(priority=prio)
            return carry

        lax.fori_loop(0, n // 2, issue, 0, unroll=4)

    def drain(slot):
        def wait(i, carry):
            copy(slot, 0, 0).wait()
            return carry

        lax.fori_loop(0, n, wait, 0, unroll=8)

    @pl.when(s == 0)
    def _():
        fetch(0, idx0_ref)

    for slot in range(2):
        @pl.when(s % 2 == slot)
        def _():
            @pl.when(s + 1 < n_tiles)
            def _():
                fetch(1 - slot, idxn_ref)

            drain(slot)
            gt = gt_ref[...]
            f = None
            for k in range(TOP_K):
                base = k * ROW_TILE * TILE_ROWS
                y = jnp.concatenate(
                    [buf_ref[slot, pl.ds(base + c, ROW_TILE, stride=TILE_ROWS), :] for c in range(TILE_ROWS)],
                    axis=-1) * gt[:, k:k + 1]
                f = y if f is None else f + y
            o_ref[...] = _layer_norm_rows(ALPHA * x_ref[...] + f, g_ref[...], b_ref[...])


def combine_ln(x, ys, dest, gates, g_all, b_all, layer):
    t, d = x.shape
    nt = t // ROW_TILE
    n = TOP_K * ROW_TILE
    row = lambda i: (i, 0)
    idx = dest.reshape(nt, ROW_TILE, TOP_K).transpose(0, 2, 1).reshape(nt, 1, n)
    return pl.pallas_call(
        _combine_ln_kernel,
        out_shape=jax.ShapeDtypeStruct((t, d), F32),
        grid=(nt,),
        in_specs=[
            pl.BlockSpec((None, 1, n), lambda i: (0, 0, 0), memory_space=pltpu.SMEM),
            pl.BlockSpec((None, 1, n), lambda i: (jnp.minimum(i + 1, nt - 1), 0, 0),
                         memory_space=pltpu.SMEM),
            pl.BlockSpec(memory_space=pl.ANY),
            pl.BlockSpec((ROW_TILE, d), row),
            pl.BlockSpec((ROW_TILE, TOP_K), row),
            pl.BlockSpec((None, 1, d), lambda i: (layer, 0, 0)),
            pl.BlockSpec((None, 1, d), lambda i: (layer, 0, 0)),
        ],
        out_specs=pl.BlockSpec((ROW_TILE, d), row),
        scratch_shapes=[pltpu.VMEM((2, n * TILE_ROWS, 128), F32),
                        pltpu.SemaphoreType.DMA((2,))],
        compiler_params=_cparams(("arbitrary",)),
        name="combine_ln",
    )(idx, idx, ys.reshape(-1, TILE_ROWS, 128), x, gates, g_all.reshape(-1, 1, d), b_all.reshape(-1, 1, d))


def moe_layer(x, x_tiles, top_e, gates, rank, counts, layer, w_gu, b_gu, w_down, b_down, ln_g, ln_b):
    t, d = x.shape
    counts = counts.reshape(N_EXPERTS).astype(jnp.int32)
    padded = (counts + MOE_TM - 1) // MOE_TM * MOE_TM
    pad_end = jnp.cumsum(padded)
    pad_start = pad_end - padded
    n_blocks = -(-t * TOP_K // MOE_TM) + N_EXPERTS
    block_start = jnp.arange(n_blocks, dtype=jnp.int32) * MOE_TM
    block_e = jnp.minimum(jnp.sum(pad_end[None, :] <= block_start[:, None], axis=1),
                          N_EXPERTS - 1).astype(jnp.int32)
    n_used = pad_end[-1] // MOE_TM
    block_valid = jnp.where(block_start < pad_end[-1],
                            jnp.clip(pad_start[block_e] + counts[block_e] - block_start, 0, MOE_TM),
                            0).astype(jnp.int32)
    after = pad_end[block_e] // MOE_TM
    next_e = jnp.where(after < n_used, block_e[jnp.minimum(after, n_blocks - 1)], -1).astype(jnp.int32)
    dest = pad_start[top_e] + rank
    xs = scatter_tokens(x_tiles, dest.reshape(-1), n_blocks * MOE_TM)
    ys = experts(xs, block_e, next_e, block_valid, w_gu, b_gu, w_down, b_down, layer)
    return combine_ln(x, ys, dest, gates, ln_g, ln_b, layer)


def kernel(x_prompt, x_sample, cache_k_diff, cache_v_diff, cache_k_swa, cache_v_swa, page_table,
           ln1_g, ln1_b, ln2_g, ln2_b, w_qkv_diff, diff_lam, diff_subln_g, w_o_diff,
           w_qkv_swa, b_qkv_swa, swa_sinks, w_o_swa, b_o_swa,
           w_router, b_router, w_gu, b_gu, w_down, b_down):
    n_batch, seq, d = x_prompt.shape
    n_seq = x_sample.shape[0]
    tp = n_batch * seq
    x = jnp.concatenate([x_prompt.reshape(tp, d), x_sample.reshape(n_seq, d)], axis=0)

    n_layers_diff, n_phys = cache_k_diff.shape[:2]
    kt_cache = jnp.transpose(cache_k_diff, (0, 1, 3, 4, 5, 2)).reshape(
        n_layers_diff, n_phys, DIFF_K_DIM, PAGE_SIZE)
    v_cache = cache_v_diff.reshape(n_layers_diff, n_phys, PAGE_SIZE * DIFF_KV_HEADS, 2 * DIFF_HEAD_DIM)
    win_buf = cache_k_swa.shape[2]
    kt_swa = jnp.transpose(cache_k_swa, (0, 1, 3, 4, 2)).reshape(-1, n_seq, SWA_KV_DIM, win_buf)
    vt_swa = jnp.transpose(cache_v_swa, (0, 1, 3, 4, 2)).reshape(-1, n_seq, SWA_KV_DIM, win_buf)

    zero_bias_qkv = jnp.zeros((1, w_qkv_diff.shape[-1]), F32)
    zero_bias_o = jnp.zeros((1, d), F32)

    kdp, vdp, kds, vds = [], [], [], []
    ksp, vsp, kss, vss = [], [], [], []
    for i in range(DEPTH):
        j = i // 2
        if i % 2 == 0:
            kq, vq = DIFF_Q_DIM, DIFF_Q_DIM + DIFF_K_DIM
            h, (k_new, v_new) = linear(x, w_qkv_diff, j, zero_bias_qkv,
                                       col_ranges=((kq, vq), (vq, vq + DIFF_V_DIM)))
            hs = h[tp:]
            a_p = diff_prompt_attention(h, diff_lam, diff_subln_g, j, i, n_batch, seq)
            a_s = diff_decode_attention(hs, kt_cache, v_cache, page_table, diff_lam, diff_subln_g, j, i)
            kdp.append(k_new[:tp].reshape(n_batch, seq, DIFF_KV_HEADS, 2, DIFF_HEAD_DIM))
            vdp.append(v_new[:tp].reshape(n_batch, seq, DIFF_KV_HEADS, 2 * DIFF_HEAD_DIM))
            kds.append(k_new[tp:].reshape(n_seq, 1, DIFF_KV_HEADS, 2, DIFF_HEAD_DIM))
            vds.append(v_new[tp:].reshape(n_seq, 1, DIFF_KV_HEADS, 2 * DIFF_HEAD_DIM))
            a = jnp.concatenate([a_p, a_s], axis=0)
            x, x_tiles, *routing = oproj_ln_route(a, w_o_diff, j, zero_bias_o, x, ln1_g, ln1_b,
                                                  w_router, b_router, i)
        else:
            h, _ = linear(x, w_qkv_swa, j, b_qkv_swa[j].reshape(1, -1))
            hs = h[tp:]
            a_p = swa_prompt_attention(h, swa_sinks[j], n_batch, seq)
            a_s = swa_decode_attention(hs, kt_swa[j], vt_swa[j], swa_sinks[j])
            kq, vq = SWA_Q_DIM, SWA_Q_DIM + SWA_KV_DIM
            keep = min(WINDOW, seq)
            hp = h[:tp].reshape(n_batch, seq, -1)[:, seq - keep:]
            ksp.append(hp[..., kq:vq].reshape(n_batch, keep, SWA_KV_HEADS, SWA_HEAD_DIM))
            vsp.append(hp[..., vq:].reshape(n_batch, keep, SWA_KV_HEADS, SWA_HEAD_DIM))
            k_new = hs[:, kq:vq].reshape(n_seq, 1, SWA_KV_HEADS, SWA_HEAD_DIM)
            v_new = hs[:, vq:].reshape(n_seq, 1, SWA_KV_HEADS, SWA_HEAD_DIM)
            kss.append(jnp.concatenate([cache_k_swa[j], k_new], axis=1)[:, -win_buf:])
            vss.append(jnp.concatenate([cache_v_swa[j], v_new], axis=1)[:, -win_buf:])
            a = jnp.concatenate([a_p, a_s], axis=0)
            x, x_tiles, *routing = oproj_ln_route(a, w_o_swa, j, b_o_swa[j].reshape(1, -1), x, ln1_g, ln1_b,
                                                  w_router, b_router, i)
        x = moe_layer(x, x_tiles, *routing, i, w_gu, b_gu, w_down, b_down, ln2_g, ln2_b)

    return (x[:tp].reshape(n_batch, seq, d), x[tp:].reshape(n_seq, 1, d),
            jnp.stack(kdp), jnp.stack(vdp), jnp.stack(kds), jnp.stack(vds),
            jnp.stack(ksp), jnp.stack(vsp), jnp.stack(kss), jnp.stack(vss))
```
